```python
import jax, jax.numpy as jnp
from jax import lax
import numpy as np

D_MODEL = 1024
BATCH = 8
SEQ = 2048
DEPTH = 4

CHUNK = 64
SUB = 16
SB_BLOCK = 128
HEAD_DIM = 128
GROUP_W = D_MODEL // 2
D_MIX = 2 * GROUP_W
H_A = GROUP_W // HEAD_DIM
H_B = GROUP_W // HEAD_DIM
H_C = GROUP_W // HEAD_DIM
DH_D = 64
H_D = GROUP_W // DH_D
CONV_A = 4
FFN_CONV = 3
D_FF = 2816
N_EVEN = (DEPTH + 1) // 2
N_ODD = DEPTH // 2
EVEN_IN = 8 * GROUP_W + 2 * H_A
ODD_IN = 7 * GROUP_W
ROPE_BASE = 10000.0
EPS = 1e-6

kernel_name = 'hybrid_streaming_encoder'


def _rms(x):
    xf = x.astype(jnp.float32)
    return xf * lax.rsqrt(jnp.mean(xf * xf, axis=-1, keepdims=True) + EPS)


def rms_norm(x, w):
    return (_rms(x) * w.astype(jnp.float32)).astype(x.dtype)


def modulate(x, shift, scale):
    return x * (1 + scale) + shift


def to_heads(x, n_heads):
    b, t, _ = x.shape
    return x.reshape(b, t, n_heads, -1).transpose(0, 2, 1, 3)


def from_heads(x):
    b, h, t, d = x.shape
    return x.transpose(0, 2, 1, 3).reshape(b, t, h * d)


def causal_dwconv(x, w, b):
    k_w, t = w.shape[0], x.shape[1]
    xp = jnp.pad(x, ((0, 0), (k_w - 1, 0), (0, 0)))
    y = b
    for i in range(k_w):
        y = y + xp[:, i:i + t] * w[i]
    return y


def rotary(x):
    t, d = x.shape[2], x.shape[3]
    inv_freq = ROPE_BASE ** (-jnp.arange(0, d, 2, dtype=jnp.float32) / d)
    ang = jnp.arange(t, dtype=jnp.float32)[:, None] * inv_freq[None, :]
    cos, sin = jnp.cos(ang), jnp.sin(ang)
    x1, x2 = x[..., : d // 2], x[..., d // 2:]
    return jnp.concatenate([x1 * cos - x2 * sin, x1 * sin + x2 * cos], axis=-1)


def _to_chunks(a):
    s = a.shape
    return jnp.moveaxis(a.reshape(s[:2] + (s[2] // CHUNK, CHUNK) + s[3:]), 2, 0)


def _from_chunks(a):
    a = jnp.moveaxis(a, 0, 2)
    s = a.shape
    return a.reshape(s[:2] + (s[2] * s[3],) + s[4:])


def gated_linear_recurrence(q, k, v, log_f):
    b_, h_, _, dk = q.shape
    dv = v.shape[-1]
    n_sub = CHUNK // SUB
    u = np.arange(SUB)
    blk = np.arange(n_sub)
    diag_mask = (blk[:, None, None, None] == blk[None, None, :, None]) & (u[None, :, None, None] >= u[None, None, None, :])
    earlier_mask = blk[None, None, :, None] < blk[:, None, None, None]

    def step(state, chunk):
        qc, kc, vc, gc = chunk
        cum = jnp.cumsum(gc, axis=2)
        cum_last = cum[:, :, -1:]
        o = jnp.einsum('bhtd,bhdv->bhtv', qc * jnp.exp(cum), state)
        ref = (cum - gc)[:, :, ::SUB]
        cum_s = cum.reshape(b_, h_, n_sub, SUB, dk)
        q_s = qc.reshape(b_, h_, n_sub, SUB, dk)
        k_s = kc.reshape(b_, h_, n_sub, SUB, dk)
        q_off = q_s * jnp.exp(cum_s - ref[:, :, :, None])
        k_off = kc[:, :, None] * jnp.exp(jnp.minimum(ref[:, :, :, None] - cum[:, :, None], 0.0))
        a_off = jnp.einsum('bhnud,bhnsd->bhnus', q_off, k_off).reshape(b_, h_, n_sub, SUB, n_sub, SUB)
        dec = jnp.exp(jnp.minimum(cum_s[:, :, :, :, None] - cum_s[:, :, :, None], 0.0))
        a_diag = jnp.einsum('bhnud,bhnsd,bhnusd->bhnus', q_s, k_s, dec)
        a = jnp.where(diag_mask, a_diag[:, :, :, :, None], jnp.where(earlier_mask, a_off, 0.0))
        o = o + jnp.einsum('bhts,bhsv->bhtv', a.reshape(b_, h_, CHUNK, CHUNK), vc)
        state = jnp.exp(cum_last[:, :, 0])[..., None] * state + jnp.einsum('bhsd,bhsv->bhdv', kc * jnp.exp(cum_last - cum), vc)
        return state, o

    s0 = jnp.zeros((b_, h_, dk, dv), jnp.float32)
    _, o = lax.scan(step, s0, (_to_chunks(q), _to_chunks(k), _to_chunks(v), _to_chunks(log_f)))
    return _from_chunks(o)


def mlstm_chunkwise(q, k, v, log_i, log_f):
    b_, h_, _, dk = q.shape
    dv = v.shape[-1]
    causal = np.tril(np.ones((CHUNK, CHUNK), dtype=bool))

    def step(carry, chunk):
        c_st, n_st, m = carry
        qc, kc, vc, ic, fc = chunk
        cum = jnp.cumsum(fc, axis=-1)
        log_d = jnp.where(causal, cum[..., :, None] - cum[..., None, :] + ic[..., None, :], -jnp.inf)
        log_inter = cum + m[..., None]
        m_t = jnp.maximum(log_inter, jnp.max(log_d, axis=-1))
        d = jnp.exp(log_d - m_t[..., None])
        w_inter = jnp.exp(log_inter - m_t)
        s = jnp.einsum('bhtd,bhsd->bhts', qc, kc) * d
        num = jnp.einsum('bhts,bhsv->bhtv', s, vc) + w_inter[..., None] * jnp.einsum('bhtd,bhdv->bhtv', qc, c_st)
        den = jnp.sum(s, axis=-1) + w_inter * jnp.einsum('bhtd,bhd->bht', qc, n_st)
        h = num / jnp.maximum(jnp.abs(den), jnp.exp(-m_t))[..., None]
        cum_last = cum[..., -1]
        log_w = cum_last[..., None] - cum + ic
        m_new = jnp.maximum(cum_last + m, jnp.max(log_w, axis=-1))
        w = jnp.exp(log_w - m_new[..., None])
        decay = jnp.exp(cum_last + m - m_new)
        c_new = decay[..., None, None] * c_st + jnp.einsum('bhs,bhsd,bhsv->bhdv', w, kc, vc)
        n_new = decay[..., None] * n_st + jnp.einsum('bhs,bhsd->bhd', w, kc)
        return (c_new, n_new, m_new), h

    carry0 = (jnp.zeros((b_, h_, dk, dv), jnp.float32), jnp.zeros((b_, h_, dk), jnp.float32), jnp.zeros((b_, h_), jnp.float32))
    _, h = lax.scan(step, carry0, (_to_chunks(q), _to_chunks(k), _to_chunks(v), _to_chunks(log_i), _to_chunks(log_f)))
    return _from_chunks(h)


def stick_breaking_attention(q, k, v):
    t_len = q.shape[2]
    scale = DH_D ** -0.5
    outs = []
    for blk in range(t_len // SB_BLOCK):
        q0, q1 = blk * SB_BLOCK, (blk + 1) * SB_BLOCK
        z = jnp.einsum('bhtd,bhsd->bhts', q[:, :, q0:q1], k[:, :, :q1]) * scale
        mask = np.arange(q1)[None, :] < np.arange(q0, q1)[:, None]
        log_rest = jnp.where(mask, jax.nn.log_sigmoid(-z), 0.0)
        tail = lax.cumsum(log_rest, axis=3, reverse=True) - log_rest
        a = jnp.where(mask, jnp.exp(jax.nn.log_sigmoid(z) + tail), 0.0)
        outs.append(jnp.einsum('bhts,bhsv->bhtv', a, v[:, :, :q1]))
    return jnp.concatenate(outs, axis=2)


def mixer_even(h, w_in, b_in, conv_w, conv_b, f_bias, norm_a, lb, norm_b, w_out):
    f32 = jnp.float32
    sizes = (2 * GROUP_W, GROUP_W, GROUP_W, H_A, H_A, GROUP_W, GROUP_W, GROUP_W, GROUP_W)
    p = h @ w_in + b_in
    qk_a, v_a, o_a, i_a, f_a, q_b, f_b, i_b, g_b = jnp.split(p, np.cumsum(sizes)[:-1].tolist(), axis=-1)
    qk_a = jax.nn.silu(causal_dwconv(qk_a, conv_w, conv_b))
    q_a, k_a = jnp.split(qk_a, 2, axis=-1)
    q_a = to_heads(q_a, H_A).astype(f32)
    k_a = to_heads(k_a, H_A).astype(f32) * HEAD_DIM ** -0.5
    v_a = to_heads(v_a, H_A).astype(f32)
    log_i = i_a.astype(f32).transpose(0, 2, 1)
    log_f = jax.nn.log_sigmoid((f_a + f_bias).astype(f32)).transpose(0, 2, 1)
    h_a = mlstm_chunkwise(q_a, k_a, v_a, log_i, log_f)
    y_a = from_heads(_rms(h_a)) * norm_a * jax.nn.sigmoid(o_a.astype(f32))
    z = to_heads(f_b, H_B).astype(f32)
    lb_h = lb.astype(f32).reshape(H_B, 1, HEAD_DIM)
    log_f_b = jnp.logaddexp(jnp.log(lb_h), jnp.log1p(-lb_h) + jax.nn.log_sigmoid(z))
    k_b = (1.0 - lb_h) * jax.nn.sigmoid(-z)
    q_bh = jax.nn.silu(to_heads(q_b, H_B).astype(f32))
    v_b = to_heads(i_b, H_B).astype(f32)
    o_b = gated_linear_recurrence(q_bh, k_b, v_b, log_f_b)
    y_b = from_heads(_rms(o_b)) * norm_b * jax.nn.silu(g_b.astype(f32))
    y = jnp.concatenate([y_a, y_b], axis=-1).astype(h.dtype)
    return y @ w_out


def mixer_odd(h, w_in, b_in, norm_c, w_out):
    f32 = jnp.float32
    p = h @ w_in + b_in
    q_c, k_c, v_c, g_c, q_d, k_d, v_d = jnp.split(p, [GROUP_W * i for i in range(1, 7)], axis=-1)
    q_ch = rotary(to_heads(q_c, H_C).astype(f32))
    k_ch = rotary(to_heads(k_c, H_C).astype(f32)) * HEAD_DIM ** -0.5
    v_ch = to_heads(v_c, H_C).astype(f32)
    log_gamma = jnp.log1p(-(2.0 ** (-5.0 - jnp.arange(H_C, dtype=f32))))
    g = jnp.broadcast_to(log_gamma[None, :, None, None], q_ch.shape)
    o_c = gated_linear_recurrence(q_ch, k_ch, v_ch, g)
    y_c = from_heads(_rms(o_c)) * norm_c * jax.nn.silu(g_c.astype(f32))
    o_d = stick_breaking_attention(to_heads(q_d, H_D).astype(f32), to_heads(k_d, H_D).astype(f32), to_heads(v_d, H_D).astype(f32))
    y = jnp.concatenate([y_c, from_heads(o_d)], axis=-1).astype(h.dtype)
    return y @ w_out


def conv_ffn(h, w_up, conv_w, conv_b, w_down):
    u = causal_dwconv(h @ w_up, conv_w, conv_b)
    a, g = jnp.split(u, 2, axis=-1)
    return (a * jax.nn.silu(g)) @ w_down


def setup_inputs(seed: int = 0) -> dict:
    key = jax.random.key(seed)
    ks = jax.random.split(key, 24)
    nrm = lambda i, shape, s: jax.random.normal(ks[i], shape, jnp.float32) * s
    d = D_MODEL
    return {
        'x': nrm(0, (BATCH, SEQ, d), 1.0),
        'c': nrm(1, (BATCH, d), 1.0),
        'w_ada': nrm(2, (d, 6 * d), d ** -0.5),
        'b_ada': nrm(3, (DEPTH, 6 * d), 0.02),
        'norm_mix': 1.0 + nrm(4, (DEPTH, d), 0.02),
        'norm_ffn': 1.0 + nrm(5, (DEPTH, d), 0.02),
        'norm_final': 1.0 + nrm(6, (d,), 0.02),
        'w_in_even': nrm(7, (N_EVEN, d, EVEN_IN), d ** -0.5),
        'b_in_even': nrm(8, (N_EVEN, EVEN_IN), 0.02),
        'conv_qk_w': nrm(9, (N_EVEN, CONV_A, 2 * GROUP_W), CONV_A ** -0.5),
        'conv_qk_b': nrm(10, (N_EVEN, 2 * GROUP_W), 0.02),
        'mlstm_f_bias': jnp.linspace(3.0, 6.0, H_A, dtype=jnp.float32)[None, :] + nrm(11, (N_EVEN, H_A), 0.1),
        'mlstm_norm': 1.0 + nrm(12, (N_EVEN, GROUP_W), 0.02),
        'hgrn_lb_logits': nrm(13, (N_EVEN, GROUP_W), 0.5),
        'hgrn_norm': 1.0 + nrm(14, (N_EVEN, GROUP_W), 0.02),
        'w_in_odd': nrm(15, (N_ODD, d, ODD_IN), d ** -0.5),
        'b_in_odd': nrm(16, (N_ODD, ODD_IN), 0.02),
        'ret_norm': 1.0 + nrm(17, (N_ODD, GROUP_W), 0.02),
        'w_out': nrm(18, (DEPTH, D_MIX, d), D_MIX ** -0.5),
        'ffn_up': nrm(19, (DEPTH, d, 2 * D_FF), d ** -0.5),
        'ffn_conv_w': nrm(20, (DEPTH, FFN_CONV, 2 * D_FF), FFN_CONV ** -0.5),
        'ffn_conv_b': nrm(21, (DEPTH, 2 * D_FF), 0.02),
        'ffn_down': nrm(22, (DEPTH, D_FF, d), D_FF ** -0.5),
    }


def reference(x, c, w_ada, b_ada, norm_mix, norm_ffn, norm_final, w_in_even, b_in_even, conv_qk_w, conv_qk_b, mlstm_f_bias, mlstm_norm, hgrn_lb_logits, hgrn_norm, w_in_odd, b_in_odd, ret_norm, w_out, ffn_up, ffn_conv_w, ffn_conv_b, ffn_down):
    ada = jax.nn.silu(c) @ w_ada
    lb_all = jnp.cumsum(jax.nn.softmax(hgrn_lb_logits.astype(jnp.float32), axis=0), axis=0)
    lb_all = lb_all - lb_all[:1]
    for layer in range(DEPTH):
        mod = (ada + b_ada[layer])[:, None, :]
        sh_m, sc_m, g_m, sh_f, sc_f, g_f = jnp.split(mod, 6, axis=-1)
        h = modulate(rms_norm(x, norm_mix[layer]), sh_m, sc_m)
        if layer % 2 == 0:
            e = layer // 2
            y = mixer_even(h, w_in_even[e], b_in_even[e], conv_qk_w[e], conv_qk_b[e], mlstm_f_bias[e], mlstm_norm[e], lb_all[e], hgrn_norm[e], w_out[layer])
        else:
            o = layer // 2
            y = mixer_odd(h, w_in_odd[o], b_in_odd[o], ret_norm[o], w_out[layer])
        x = x + g_m * y
        h = modulate(rms_norm(x, norm_ffn[layer]), sh_f, sc_f)
        x = x + g_f * conv_ffn(h, ffn_up[layer], ffn_conv_w[layer], ffn_conv_b[layer], ffn_down[layer])
    return rms_norm(x, norm_final)
```

```python
import functools

import numpy as np
import jax
import jax.numpy as jnp
from jax import lax
from jax.experimental import pallas as pl
from jax.experimental.pallas import tpu as pltpu

F32 = jnp.float32
BF16 = jnp.bfloat16

D_MODEL = 1024
DEPTH = 4
HEAD_DIM = 128
GROUP_W = D_MODEL // 2
N_HEADS = GROUP_W // HEAD_DIM
SB_HEAD_DIM = 64
CONV_A = 4
FFN_CONV = 3
D_FF = 2816
ROPE_BASE = 10000.0
EPS = 1e-6

LANES = 128
SUBLANES = 8
REC_CHUNK = 128
SB_TILE = 256
TOK_TILE = 512
FF_CHUNK = 256
VMEM_LIMIT = 60 * 1024 * 1024
NEG_BIG = -1e30


def _dot(a, b):
    return jnp.dot(a, b, preferred_element_type=F32)


def _dot_tb(a, b):
    return lax.dot_general(a, b, (((1,), (1,)), ((), ())), preferred_element_type=F32)


def _dot_ta(a, b):
    return lax.dot_general(a, b, (((0,), (0,)), ((), ())), preferred_element_type=F32)


def _split3(x):
    hi = x.astype(BF16)
    r1 = x - hi.astype(F32)
    mid = r1.astype(BF16)
    lo = (r1 - mid.astype(F32)).astype(BF16)
    return hi, mid, lo


def _dot_exact_lhs01(w01, x):
    hi, mid, lo = _split3(x)
    return _dot(w01, hi) + _dot(w01, mid) + _dot(w01, lo)


def _sigmoid(x):
    return jax.nn.sigmoid(x)


def _silu(x):
    return x * jax.nn.sigmoid(x)


def _log_sigmoid(x):
    return jnp.minimum(x, 0.0) - jnp.log1p(jnp.exp(-jnp.abs(x)))


def _rms_lanes(x):
    return x * lax.rsqrt(jnp.mean(x * x, axis=-1, keepdims=True) + EPS)


def _cparams(*sem):
    return pltpu.CompilerParams(dimension_semantics=sem, vmem_limit_bytes=VMEM_LIMIT)


def _const_spec(shape):
    nd = len(shape)
    return pl.BlockSpec(shape, lambda *_: (0,) * nd, pipeline_mode=pl.Buffered(1))


def _ada_kernel(c_ref, w_ref, b_ref, o_ref):
    c = c_ref[...]
    ada = jnp.dot(_silu(c), w_ref[...], preferred_element_type=F32,
                  precision=lax.Precision.HIGHEST)
    for layer in range(DEPTH):
        o_ref[layer] = ada + b_ref[layer:layer + 1, :]


def _ada(c, w_ada, b_ada):
    bsz, d = c.shape
    n = w_ada.shape[1]
    tn = 1024
    return pl.pallas_call(
        _ada_kernel,
        grid=(n // tn,),
        in_specs=[pl.BlockSpec((bsz, d), lambda j: (0, 0)),
                  pl.BlockSpec((d, tn), lambda j: (0, j)),
                  pl.BlockSpec((DEPTH, tn), lambda j: (0, j))],
        out_specs=pl.BlockSpec((DEPTH, bsz, tn), lambda j: (0, 0, j)),
        out_shape=jax.ShapeDtypeStruct((DEPTH, bsz, n), F32),
        compiler_params=_cparams("arbitrary"),
        name="ada_mod",
    )(c, w_ada, b_ada)


def _modulated_norm(x, nw, shift, scale):
    return _rms_lanes(x) * nw * (1.0 + scale) + shift


def _inproj_kernel(*refs, even, n_cols, tm):
    if even:
        (x_ref, mod_ref, nw_ref, w_ref, b_ref, wg_ref, bg_ref, cw_ref, cb_ref,
         p_ref, g_ref, ext_ref) = refs
    else:
        x_ref, mod_ref, nw_ref, w_ref, b_ref, p_ref = refs
    mod = mod_ref[0]
    h = _modulated_norm(x_ref[0], nw_ref[...], mod[0:1], mod[1:2])
    hb = h.astype(BF16)
    col_chunk = 4 * LANES
    conv_cols = 2 * GROUP_W

    if even:
        @pl.when(pl.program_id(1) == 0)
        def _():
            ext_ref[0:SUBLANES, :] = jnp.zeros((SUBLANES, conv_cols), F32)

    for j in range(n_cols // col_chunk):
        cols = slice(j * col_chunk, (j + 1) * col_chunk)
        res = _dot(hb, w_ref[:, cols]) + b_ref[:, cols]
        if even and (j + 1) * col_chunk <= conv_cols:
            ext_ref[SUBLANES:SUBLANES + tm, cols] = res
            acc = cb_ref[:, cols]
            for i in range(CONV_A):
                off = SUBLANES - (CONV_A - 1) + i
                acc = acc + cw_ref[i:i + 1, cols] * ext_ref[off:off + tm, cols]
            ext_ref[0:SUBLANES, cols] = ext_ref[tm:tm + SUBLANES, cols]
            res = _silu(acc)
        for i in range(col_chunk // LANES):
            p_ref[0, j * (col_chunk // LANES) + i] = res[:, i * LANES:(i + 1) * LANES]

    if even:
        g = _dot(hb, wg_ref[...]) + bg_ref[...]
        for i in range(N_HEADS):
            g_ref[0, i] = g[:, i * LANES:(i + 1) * LANES]


def _inproj(x, mod_l, nw, w, b, even, gate_w=None, gate_b=None, conv_w=None, conv_b=None):
    bsz, t, d = x.shape
    n_cols = w.shape[1]
    nblk = n_cols // LANES
    tm = min(TOK_TILE, t)
    in_specs = [pl.BlockSpec((1, tm, d), lambda bi, ti: (bi, ti, 0)),
                pl.BlockSpec((1, 6, d), lambda bi, ti: (bi, 0, 0)),
                _const_spec((1, d)),
                _const_spec((d, n_cols)),
                _const_spec((1, n_cols))]
    args = [x, mod_l, nw, w, b]
    out_specs = [pl.BlockSpec((1, nblk, tm, LANES), lambda bi, ti: (bi, 0, ti, 0))]
    out_shape = [jax.ShapeDtypeStruct((bsz, nblk, t, LANES), F32)]
    scratch = []
    if even:
        in_specs += [_const_spec((d, N_HEADS * LANES)), _const_spec((1, N_HEADS * LANES)),
                     _const_spec((CONV_A, 2 * GROUP_W)), _const_spec((1, 2 * GROUP_W))]
        args += [gate_w, gate_b, conv_w, conv_b]
        out_specs.append(pl.BlockSpec((1, N_HEADS, tm, LANES), lambda bi, ti: (bi, 0, ti, 0)))
        out_shape.append(jax.ShapeDtypeStruct((bsz, N_HEADS, t, LANES), F32))
        scratch.append(pltpu.VMEM((tm + SUBLANES, 2 * GROUP_W), F32))
    return pl.pallas_call(
        functools.partial(_inproj_kernel, even=even, n_cols=n_cols, tm=tm),
        grid=(bsz, t // tm),
        in_specs=in_specs,
        out_specs=out_specs,
        out_shape=out_shape,
        scratch_shapes=scratch,
        compiler_params=_cparams("arbitrary", "arbitrary"),
        name="inproj_even" if even else "inproj_odd",
    )(*args)


def _mlstm_kernel(q_ref, k_ref, v_ref, o_ref, g_ref, fb_ref, nw_ref, tri_ref, y_ref, *, seq):
    L = REC_CHUNK
    row = lax.broadcasted_iota(jnp.int32, (L, L), 0)
    col = lax.broadcasted_iota(jnp.int32, (L, L), 1)
    causal = col <= row
    lane0 = lax.broadcasted_iota(jnp.int32, (L, LANES), 1) == 0
    ones = jnp.ones((L, HEAD_DIM), BF16)

    def body(c, carry):
        c_aug, m_prev = carry
        r0 = pl.multiple_of(c * L, L)
        rows = pl.ds(r0, L)
        q = q_ref[0, 0, rows, :]
        k = k_ref[0, 0, rows, :] * (HEAD_DIM ** -0.5)
        v = v_ref[0, 0, rows, :]
        gates = g_ref[0, 0, rows, :] + fb_ref[0]
        cum_t = _dot_exact_lhs01(tri_ref[...], _log_sigmoid(gates))
        cum = cum_t[:, 1:2]
        a_col = gates[:, 0:1] - cum
        a_row = jnp.where(lane0, a_col, 0.0).T[0:1, :]
        a_b = jnp.broadcast_to(a_row, (L, L))
        g = jnp.maximum(jnp.max(jnp.where(causal, a_b, NEG_BIG), axis=1, keepdims=True), m_prev)
        dmat = jnp.exp(jnp.where(causal, a_b - g, NEG_BIG))
        qb = q.astype(BF16)
        v_aug = jnp.concatenate([v.astype(BF16), ones], axis=1)
        s = _dot_tb(qb, k.astype(BF16)) * dmat
        r = _dot(s.astype(BF16), v_aug) + jnp.exp(m_prev - g) * _dot(qb, c_aug.astype(BF16))
        num = r[:, :HEAD_DIM]
        den = r[:, HEAD_DIM:]
        h = num / jnp.maximum(jnp.abs(den), jnp.exp(-(cum + g)))
        y = _rms_lanes(h) * nw_ref[0] * _sigmoid(o_ref[0, 0, rows, :])
        y_ref[0, rows, :] = y.astype(y_ref.dtype)
        g_last = g[L - 1:L, :]
        kw = (k * jnp.exp(a_col - g_last)).astype(BF16)
        c_new = jnp.exp(m_prev - g_last) * c_aug + _dot_ta(kw, v_aug)
        return c_new, cum[L - 1:L, :] + g_last

    init = (jnp.zeros((HEAD_DIM, 2 * HEAD_DIM), F32), jnp.zeros((1, 1), F32))
    lax.fori_loop(0, seq // L, body, init)


def _mlstm(p, gates, f_bias_rows, norm_rows, tri):
    bsz, _, t, _ = p.shape

    def hspec(off):
        return pl.BlockSpec((1, 1, t, LANES), lambda bi, hi: (bi, off + hi, 0, 0))

    return pl.pallas_call(
        functools.partial(_mlstm_kernel, seq=t),
        grid=(bsz, N_HEADS),
        in_specs=[hspec(0), hspec(N_HEADS), hspec(2 * N_HEADS), hspec(3 * N_HEADS),
                  pl.BlockSpec((1, 1, t, LANES), lambda bi, hi: (bi, hi, 0, 0)),
                  pl.BlockSpec((1, 1, LANES), lambda bi, hi: (hi, 0, 0)),
                  pl.BlockSpec((1, 1, LANES), lambda bi, hi: (hi, 0, 0)),
                  _const_spec((REC_CHUNK, REC_CHUNK))],
        out_specs=pl.BlockSpec((1, t, LANES), lambda bi, hi: (bi, 0, hi)),
        out_shape=jax.ShapeDtypeStruct((bsz, t, GROUP_W), BF16),
        compiler_params=_cparams("arbitrary", "arbitrary"),
        name="mlstm",
    )(p, p, p, p, gates, f_bias_rows, norm_rows, tri)


def _hgrn_tables():
    L = REC_CHUNK
    t = np.arange(L)
    blocks = [t[None, :] <= t[:, None],
              t[None, :] > t[:, None]]
    masks = []
    lev = 0
    while (1 << lev) < L:
        h = 1 << lev
        blk = t // (2 * h)
        upper = (t % (2 * h)) >= h
        mid = blk * 2 * h + h
        eq = upper[:, None] & (t[None, :] >= mid[:, None]) & (t[None, :] <= t[:, None])
        ek = (~upper)[:, None] & (t[None, :] > t[:, None]) & (t[None, :] < mid[:, None])
        blocks += [eq, ek]
        masks.append((blk[:, None] == blk[None, :]) & upper[:, None] & (~upper)[None, :])
        lev += 1
    masks.append(t[:, None] == t[None, :])
    w = np.concatenate(blocks, axis=0).astype(np.float32)
    m = np.stack(masks).astype(np.float32)
    return w, m, lev


def _hgrn_kernel(q_ref, f_ref, i_ref, g_ref, lbl_ref, nw_ref, w_ref, m_ref, y_ref,
                 *, seq, n_lev, layer_e):
    L = REC_CHUNK
    sm = jax.nn.softmax(lbl_ref[:, 0, :], axis=0)
    lb = jnp.zeros((1, HEAD_DIM), F32)
    for r in range(1, layer_e + 1):
        lb = lb + sm[r:r + 1, :]
    lb_pos = lb > 0.0
    log_lb = jnp.log(jnp.where(lb_pos, lb, 1.0))
    log_1m = jnp.log1p(-lb)

    def body(c, st):
        r0 = pl.multiple_of(c * L, L)
        rows = pl.ds(r0, L)
        z = f_ref[0, 0, rows, :]
        q = _silu(q_ref[0, 0, rows, :])
        v = i_ref[0, 0, rows, :].astype(BF16)
        ls = log_1m + _log_sigmoid(z)
        lae = jnp.maximum(log_lb, ls) + jnp.log1p(jnp.exp(-jnp.abs(log_lb - ls)))
        log_f = jnp.where(lb_pos, lae, ls)
        k = (1.0 - lb) * _sigmoid(-z)
        dec = jnp.exp(_dot_exact_lhs01(w_ref[...], log_f))
        o = _dot_tb((q * dec[0:L]).astype(BF16), st.astype(BF16))
        a = _dot_tb(q.astype(BF16), k.astype(BF16)) * m_ref[n_lev]
        for lev in range(n_lev):
            qs = (q * dec[(2 + 2 * lev) * L:(3 + 2 * lev) * L]).astype(BF16)
            ks = (k * dec[(3 + 2 * lev) * L:(4 + 2 * lev) * L]).astype(BF16)
            a = a + _dot_tb(qs, ks) * m_ref[lev]
        o = o + _dot(a.astype(BF16), v)
        y = _rms_lanes(o) * nw_ref[0] * _silu(g_ref[0, 0, rows, :])
        y_ref[0, rows, :] = y.astype(y_ref.dtype)
        kend = (k * dec[L:2 * L]).astype(BF16)
        return dec[L - 1:L, :] * st + _dot_ta(v, kend)

    lax.fori_loop(0, seq // L, body, jnp.zeros((HEAD_DIM, HEAD_DIM), F32))


def _hgrn(p, lb_logits_rows, norm_rows, layer_e):
    bsz, _, t, _ = p.shape
    w_np, m_np, n_lev = _hgrn_tables()
    w_tab = jnp.asarray(w_np, BF16)
    m_tab = jnp.asarray(m_np, F32)
    n_even = lb_logits_rows.shape[0]

    def hspec(off):
        return pl.BlockSpec((1, 1, t, LANES), lambda bi, hi: (bi, off + hi, 0, 0))

    return pl.pallas_call(
        functools.partial(_hgrn_kernel, seq=t, n_lev=n_lev, layer_e=layer_e),
        grid=(bsz, N_HEADS),
        in_specs=[hspec(4 * N_HEADS), hspec(5 * N_HEADS), hspec(6 * N_HEADS), hspec(7 * N_HEADS),
                  pl.BlockSpec((n_even, 1, LANES), lambda bi, hi: (0, 0, hi)),
                  pl.BlockSpec((1, 1, LANES), lambda bi, hi: (hi, 0, 0)),
                  _const_spec(w_tab.shape), _const_spec(m_tab.shape)],
        out_specs=pl.BlockSpec((1, t, LANES), lambda bi, hi: (bi, 0, hi)),
        out_shape=jax.ShapeDtypeStruct((bsz, t, GROUP_W), BF16),
        compiler_params=_cparams("arbitrary", "arbitrary"),
        name="hgrn2",
    )(p, p, p, p, lb_logits_rows, norm_rows, w_tab, m_tab)


def _retention_kernel(q_ref, k_ref, v_ref, g_ref, cos_ref, sin_ref, lg_ref, nw_ref, y_ref, *, seq):
    L = REC_CHUNK
    lg = lg_ref[0]
    row = lax.broadcasted_iota(jnp.int32, (L, L), 0)
    col = lax.broadcasted_iota(jnp.int32, (L, L), 1)
    dist = (row - col).astype(F32)
    dmat = jnp.where(col <= row, jnp.exp(dist * lg[:, 0:1]), 0.0)
    pos = lax.broadcasted_iota(jnp.int32, (L, HEAD_DIM), 0).astype(F32)
    dec_q = jnp.exp((pos + 1.0) * lg)
    dec_k = jnp.exp((L - 1.0 - pos) * lg)
    dec_all = jnp.exp(L * lg)

    def rot(x, cs, sn):
        return x * cs + pltpu.roll(x, HEAD_DIM // 2, 1) * sn

    def body(c, st):
        r0 = pl.multiple_of(c * L, L)
        rows = pl.ds(r0, L)
        cs = cos_ref[rows, :]
        sn = sin_ref[rows, :]
        q = rot(q_ref[0, 0, rows, :], cs, sn)
        k = rot(k_ref[0, 0, rows, :], cs, sn) * (HEAD_DIM ** -0.5)
        v = v_ref[0, 0, rows, :].astype(BF16)
        s = _dot_tb(q.astype(BF16), k.astype(BF16)) * dmat
        o = _dot(s.astype(BF16), v) + _dot_tb((q * dec_q).astype(BF16), st.astype(BF16))
        y = _rms_lanes(o) * nw_ref[0] * _silu(g_ref[0, 0, rows, :])
        y_ref[0, rows, :] = y.astype(y_ref.dtype)
        return dec_all * st + _dot_ta(v, (k * dec_k).astype(BF16))

    lax.fori_loop(0, seq // L, body, jnp.zeros((HEAD_DIM, HEAD_DIM), F32))


def _retention(p, norm_rows):
    bsz, _, t, _ = p.shape
    half = HEAD_DIM // 2
    inv_freq = ROPE_BASE ** (-np.arange(0, HEAD_DIM, 2, dtype=np.float32) / HEAD_DIM)
    ang = jnp.arange(t, dtype=F32)[:, None] * jnp.asarray(inv_freq, F32)[None, :]
    cos, sin = jnp.cos(ang), jnp.sin(ang)
    cos2 = jnp.concatenate([cos, cos], axis=-1)
    sin2 = jnp.concatenate([-sin, sin], axis=-1)
    assert cos2.shape == (t, 2 * half)
    log_gamma = np.log1p(-(2.0 ** (-5.0 - np.arange(N_HEADS, dtype=np.float32)))).astype(np.float32)
    lg_rows = jnp.asarray(np.broadcast_to(log_gamma[:, None, None], (N_HEADS, 1, LANES)).copy())

    def hspec(off):
        return pl.BlockSpec((1, 1, t, LANES), lambda bi, hi: (bi, off + hi, 0, 0))

    return pl.pallas_call(
        functools.partial(_retention_kernel, seq=t),
        grid=(bsz, N_HEADS),
        in_specs=[hspec(0), hspec(N_HEADS), hspec(2 * N_HEADS), hspec(3 * N_HEADS),
                  _const_spec((t, LANES)), _const_spec((t, LANES)),
                  pl.BlockSpec((1, 1, LANES), lambda bi, hi: (hi, 0, 0)),
                  pl.BlockSpec((1, 1, LANES), lambda bi, hi: (hi, 0, 0))],
        out_specs=pl.BlockSpec((1, t, LANES), lambda bi, hi: (bi, 0, hi)),
        out_shape=jax.ShapeDtypeStruct((bsz, t, GROUP_W), BF16),
        compiler_params=_cparams("arbitrary", "arbitrary"),
        name="retention",
    )(p, p, p, p, cos2, sin2, lg_rows, norm_rows)


def _sb_kernel(q_ref, k_ref, v_ref, su_ref, y_ref):
    tq = SB_TILE
    qi = pl.program_id(2)
    q2 = q_ref[0, 0] * (SB_HEAD_DIM ** -0.5)
    first = lax.broadcasted_iota(jnp.int32, (tq, LANES), 1) < SB_HEAD_DIM
    q_heads = (jnp.where(first, q2, 0.0).astype(BF16), jnp.where(first, 0.0, q2).astype(BF16))
    row = lax.broadcasted_iota(jnp.int32, (tq, tq), 0)
    col = lax.broadcasted_iota(jnp.int32, (tq, tq), 1)
    strict = col < row

    def block(j, state, diag):
        r0 = pl.multiple_of(j * tq, tq)
        k2 = k_ref[0, 0, pl.ds(r0, tq), :].astype(BF16)
        v2 = v_ref[0, 0, pl.ds(r0, tq), :].astype(BF16)
        new = []
        for hh in range(2):
            carry, acc = state[hh]
            z = _dot_tb(q_heads[hh], k2)
            sp = jnp.maximum(z, 0.0) + jnp.log1p(jnp.exp(-jnp.abs(z)))
            log_rest = -sp
            if diag:
                log_rest = jnp.where(strict, log_rest, 0.0)
            hi = log_rest.astype(BF16)
            mid = (log_rest - hi.astype(F32)).astype(BF16)
            tail = _dot(hi, su_ref[...]) + _dot(mid, su_ref[...])
            arg = (z - sp) + tail + carry
            if diag:
                arg = jnp.where(strict, arg, NEG_BIG)
            acc = acc + _dot(jnp.exp(arg).astype(BF16), v2)
            carry = carry + tail[:, 0:1] + log_rest[:, 0:1]
            new.append((carry, acc))
        return tuple(new)

    zero = (jnp.zeros((tq, 1), F32), jnp.zeros((tq, LANES), F32))
    state = block(qi, (zero, zero), True)
    state = lax.fori_loop(1, qi + 1, lambda kb, st: block(qi - kb, st, False), state)
    y_ref[0] = jnp.where(first, state[0][1], state[1][1]).astype(y_ref.dtype)


def _stick_breaking(p):
    bsz, _, t, _ = p.shape
    tq = SB_TILE
    n_pair = GROUP_W // LANES
    idx = np.arange(tq)
    su = jnp.asarray((idx[:, None] > idx[None, :]).astype(np.float32), BF16)
    return pl.pallas_call(
        _sb_kernel,
        grid=(bsz, n_pair, t // tq),
        in_specs=[pl.BlockSpec((1, 1, tq, LANES), lambda bi, hi, qi: (bi, 4 * N_HEADS + hi, qi, 0)),
                  pl.BlockSpec((1, 1, t, LANES), lambda bi, hi, qi: (bi, 4 * N_HEADS + n_pair + hi, 0, 0)),
                  pl.BlockSpec((1, 1, t, LANES), lambda bi, hi, qi: (bi, 4 * N_HEADS + 2 * n_pair + hi, 0, 0)),
                  _const_spec((tq, tq))],
        out_specs=pl.BlockSpec((1, tq, LANES), lambda bi, hi, qi: (bi, qi, hi)),
        out_shape=jax.ShapeDtypeStruct((bsz, t, GROUP_W), BF16),
        compiler_params=_cparams("arbitrary", "arbitrary", "arbitrary"),
        name="stick_breaking",
    )(p, p, p, su)


def _ffn_kernel(*refs, tm, final):
    if final:
        (x_ref, y1_ref, y2_ref, mod_ref, nw_ref, wo_ref, wu_ref, cw_ref, cb_ref, wd_ref, nf_ref,
         o_ref, ext_ref, carry_ref) = refs
    else:
        (x_ref, y1_ref, y2_ref, mod_ref, nw_ref, wo_ref, wu_ref, cw_ref, cb_ref, wd_ref,
         o_ref, ext_ref, carry_ref) = refs
    fc = FF_CHUNK
    n_chunks = D_FF // fc

    @pl.when(pl.program_id(1) == 0)
    def _():
        carry_ref[...] = jnp.zeros(carry_ref.shape, F32)

    mod = mod_ref[0]
    mix = _dot(y1_ref[0], wo_ref[0:GROUP_W, :]) + _dot(y2_ref[0], wo_ref[GROUP_W:, :])
    x1 = x_ref[0] + mod[2:3] * mix
    hb = _modulated_norm(x1, nw_ref[...], mod[3:4], mod[4:5]).astype(BF16)
    acc = jnp.zeros((tm, D_MODEL), F32)
    for j in range(n_chunks):
        halves = []
        for part in range(2):
            cols = slice(part * D_FF + j * fc, part * D_FF + (j + 1) * fc)
            ext_ref[0:SUBLANES, :] = carry_ref[2 * j + part]
            ext_ref[SUBLANES:SUBLANES + tm, :] = _dot(hb, wu_ref[:, cols])
            carry_ref[2 * j + part] = ext_ref[tm:tm + SUBLANES, :]
            u = cb_ref[:, cols]
            for i in range(FFN_CONV):
                off = SUBLANES - (FFN_CONV - 1) + i
                u = u + cw_ref[i:i + 1, cols] * ext_ref[off:off + tm, :]
            halves.append(u)
        act = (halves[0] * _silu(halves[1])).astype(BF16)
        acc = acc + _dot(act, wd_ref[j * fc:(j + 1) * fc, :])
    x2 = x1 + mod[5:6] * acc
    if final:
        x2 = _rms_lanes(x2) * nf_ref[...]
    o_ref[0] = x2


def _ffn(x, y1, y2, mod_l, nw, w_out, w_up, conv_w, conv_b, w_down, norm_final=None):
    bsz, t, d = x.shape
    tm = min(TOK_TILE, t)
    final = norm_final is not None
    in_specs = [pl.BlockSpec((1, tm, d), lambda bi, ti: (bi, ti, 0)),
                pl.BlockSpec((1, tm, GROUP_W), lambda bi, ti: (bi, ti, 0)),
                pl.BlockSpec((1, tm, GROUP_W), lambda bi, ti: (bi, ti, 0)),
                pl.BlockSpec((1, 6, d), lambda bi, ti: (bi, 0, 0)),
                _const_spec((1, d)),
                _const_spec(w_out.shape), _const_spec(w_up.shape),
                _const_spec(conv_w.shape), _const_spec(conv_b.shape),
                _const_spec(w_down.shape)]
    args = [x, y1, y2, mod_l, nw, w_out, w_up, conv_w, conv_b, w_down]
    if final:
        in_specs.append(_const_spec((1, d)))
        args.append(norm_final)
    return pl.pallas_call(
        functools.partial(_ffn_kernel, tm=tm, final=final),
        grid=(bsz, t // tm),
        in_specs=in_specs,
        out_specs=pl.BlockSpec((1, tm, d), lambda bi, ti: (bi, ti, 0)),
        out_shape=jax.ShapeDtypeStruct((bsz, t, d), F32),
        scratch_shapes=[pltpu.VMEM((tm + SUBLANES, FF_CHUNK), F32),
                        pltpu.VMEM((2 * (D_FF // FF_CHUNK), SUBLANES, FF_CHUNK), F32)],
        compiler_params=_cparams("arbitrary", "arbitrary"),
        name="outproj_ffn_final" if final else "outproj_ffn",
    )(*args)


def _head_rows(vec):
    return vec.reshape(-1, 1, LANES)


def kernel(x, c, w_ada, b_ada, norm_mix, norm_ffn, norm_final, w_in_even, b_in_even, conv_qk_w,
           conv_qk_b, mlstm_f_bias, mlstm_norm, hgrn_lb_logits, hgrn_norm, w_in_odd, b_in_odd,
           ret_norm, w_out, ffn_up, ffn_conv_w, ffn_conv_b, ffn_down):
    bsz, t, d = x.shape
    assert d == D_MODEL and t % SB_TILE == 0 and t % REC_CHUNK == 0
    mod = _ada(c, w_ada, b_ada).reshape(DEPTH, bsz, 6, d)
    idx = np.arange(REC_CHUNK)
    tri = jnp.asarray((idx[None, :] <= idx[:, None]).astype(np.float32), BF16)
    gate0 = 4 * GROUP_W
    n_even = hgrn_lb_logits.shape[0]
    lb_rows = hgrn_lb_logits.astype(F32).reshape(n_even, 1, GROUP_W)

    for layer in range(DEPTH):
        mod_l = mod[layer]
        nw_mix = norm_mix[layer][None, :]
        if layer % 2 == 0:
            e = layer // 2
            w, b = w_in_even[e], b_in_even[e]
            w_main = jnp.concatenate([w[:, :gate0], w[:, gate0 + 2 * N_HEADS:]], axis=1).astype(BF16)
            b_main = jnp.concatenate([b[:gate0], b[gate0 + 2 * N_HEADS:]])[None, :]
            gw = jnp.zeros((d, N_HEADS, LANES), F32)
            gw = gw.at[:, :, 0].set(w[:, gate0:gate0 + N_HEADS])
            gw = gw.at[:, :, 1].set(w[:, gate0 + N_HEADS:gate0 + 2 * N_HEADS])
            gb = jnp.zeros((N_HEADS, LANES), F32)
            gb = gb.at[:, 0].set(b[gate0:gate0 + N_HEADS]).at[:, 1].set(b[gate0 + N_HEADS:gate0 + 2 * N_HEADS])
            p, gates = _inproj(x, mod_l, nw_mix, w_main, b_main, True,
                               gw.reshape(d, N_HEADS * LANES).astype(BF16), gb.reshape(1, -1),
                               conv_qk_w[e], conv_qk_b[e][None, :])
            fb_rows = jnp.zeros((N_HEADS, 1, LANES), F32).at[:, 0, 1].set(mlstm_f_bias[e])
            y1 = _mlstm(p, gates, fb_rows, _head_rows(mlstm_norm[e]), tri)
            y2 = _hgrn(p, lb_rows, _head_rows(hgrn_norm[e]), e)
        else:
            o = layer // 2
            (p,) = _inproj(x, mod_l, nw_mix, w_in_odd[o].astype(BF16), b_in_odd[o][None, :], False)
            y1 = _retention(p, _head_rows(ret_norm[o]))
            y2 = _stick_breaking(p)
        x = _ffn(x, y1, y2, mod_l, norm_ffn[layer][None, :], w_out[layer].astype(BF16),
                 ffn_up[layer].astype(BF16), ffn_conv_w[layer], ffn_conv_b[layer][None, :],
                 ffn_down[layer].astype(BF16),
                 norm_final[None, :] if layer == DEPTH - 1 else None)
    return x
```

```python
import functools

import numpy as np
import jax
import jax.numpy as jnp
from jax import lax
from jax.experimental import pallas as pl
from jax.experimental.pallas import tpu as pltpu

F32 = jnp.float32
BF16 = jnp.bfloat16

D_MODEL = 1024
DEPTH = 4
HEAD_DIM = 128
GROUP_W = D_MODEL // 2
N_HEADS = GROUP_W // HEAD_DIM
SB_HEAD_DIM = 64
CONV_A = 4
FFN_CONV = 3
D_FF = 2816
ROPE_BASE = 10000.0
EPS = 1e-6

LANES = 128
SUBLANES = 8
REC_CHUNK = 128
MLSTM_HEADS_PER_STEP = 4
HGRN_HEADS_PER_STEP = 2
RET_HEADS_PER_STEP = 4
SB_Q_TILE = 512
SB_K_TILE = 256
SB_STRIP = 16
TOK_TILE = 512
FF_CHUNK = 256
VMEM_LIMIT = 60 * 1024 * 1024
NEG_BIG = -1e30


def _dot(a, b):
    return jnp.dot(a, b, preferred_element_type=F32)


def _dot_tb(a, b):
    return lax.dot_general(a, b, (((1,), (1,)), ((), ())), preferred_element_type=F32)


def _dot_ta(a, b):
    return lax.dot_general(a, b, (((0,), (0,)), ((), ())), preferred_element_type=F32)


def _split3(x):
    hi = x.astype(BF16)
    r1 = x - hi.astype(F32)
    mid = r1.astype(BF16)
    lo = (r1 - mid.astype(F32)).astype(BF16)
    return hi, mid, lo


def _dot_lhs01(w01, x, terms):
    parts = _split3(x)[:terms]
    out = _dot(w01, parts[0])
    for part in parts[1:]:
        out = out + _dot(w01, part)
    return out


def _sigmoid(x):
    return jax.nn.sigmoid(x)


def _silu(x):
    return x * jax.nn.sigmoid(x)


def _log_sigmoid(x):
    return jnp.minimum(x, 0.0) - jnp.log1p(jnp.exp(-jnp.abs(x)))


def _rms_lanes(x):
    return x * lax.rsqrt(jnp.mean(x * x, axis=-1, keepdims=True) + EPS)


def _cparams(*sem):
    return pltpu.CompilerParams(dimension_semantics=sem, vmem_limit_bytes=VMEM_LIMIT)


def _const_spec(shape):
    nd = len(shape)
    return pl.BlockSpec(shape, lambda *_: (0,) * nd, pipeline_mode=pl.Buffered(1))


def _ada_kernel(c_ref, w_ref, b_ref, o_ref):
    c = c_ref[...]
    ada = jnp.dot(_silu(c), w_ref[...], preferred_element_type=F32,
                  precision=lax.Precision.HIGHEST)
    for layer in range(DEPTH):
        o_ref[layer] = ada + b_ref[layer:layer + 1, :]


def _ada(c, w_ada, b_ada):
    bsz, d = c.shape
    n = w_ada.shape[1]
    tn = 1024
    return pl.pallas_call(
        _ada_kernel,
        grid=(n // tn,),
        in_specs=[pl.BlockSpec((bsz, d), lambda j: (0, 0)),
                  pl.BlockSpec((d, tn), lambda j: (0, j)),
                  pl.BlockSpec((DEPTH, tn), lambda j: (0, j))],
        out_specs=pl.BlockSpec((DEPTH, bsz, tn), lambda j: (0, 0, j)),
        out_shape=jax.ShapeDtypeStruct((DEPTH, bsz, n), F32),
        compiler_params=_cparams("arbitrary"),
        name="ada_mod",
    )(c, w_ada, b_ada)


def _modulated_norm(x, nw, shift, scale):
    return _rms_lanes(x) * nw * (1.0 + scale) + shift


def _inproj_kernel(*refs, even, n_cols, tm):
    if even:
        (x_ref, mod_ref, nw_ref, w_ref, b_ref, wg_ref, bg_ref, cw_ref, cb_ref,
         p_ref, g_ref, ext_ref) = refs
    else:
        x_ref, mod_ref, nw_ref, w_ref, b_ref, p_ref = refs
    mod = mod_ref[0]
    h = _modulated_norm(x_ref[0], nw_ref[...], mod[0:1], mod[1:2])
    hb = h.astype(BF16)
    col_chunk = 4 * LANES
    conv_cols = 2 * GROUP_W

    if even:
        @pl.when(pl.program_id(1) == 0)
        def _():
            ext_ref[0:SUBLANES, :] = jnp.zeros((SUBLANES, conv_cols), F32)

    for j in range(n_cols // col_chunk):
        cols = slice(j * col_chunk, (j + 1) * col_chunk)
        res = _dot(hb, w_ref[:, cols]) + b_ref[:, cols]
        if even and (j + 1) * col_chunk <= conv_cols:
            ext_ref[SUBLANES:SUBLANES + tm, cols] = res
            acc = cb_ref[:, cols]
            for i in range(CONV_A):
                off = SUBLANES - (CONV_A - 1) + i
                acc = acc + cw_ref[i:i + 1, cols] * ext_ref[off:off + tm, cols]
            ext_ref[0:SUBLANES, cols] = ext_ref[tm:tm + SUBLANES, cols]
            res = _silu(acc)
        for i in range(col_chunk // LANES):
            p_ref[0, j * (col_chunk // LANES) + i] = res[:, i * LANES:(i + 1) * LANES]

    if even:
        g = _dot(hb, wg_ref[...]) + bg_ref[...]
        for i in range(N_HEADS):
            g_ref[0, i] = g[:, i * LANES:(i + 1) * LANES]


def _inproj(x, mod_l, nw, w, b, even, gate_w=None, gate_b=None, conv_w=None, conv_b=None):
    bsz, t, d = x.shape
    n_cols = w.shape[1]
    nblk = n_cols // LANES
    tm = min(TOK_TILE, t)
    in_specs = [pl.BlockSpec((1, tm, d), lambda bi, ti: (bi, ti, 0)),
                pl.BlockSpec((1, 6, d), lambda bi, ti: (bi, 0, 0)),
                _const_spec((1, d)),
                _const_spec((d, n_cols)),
                _const_spec((1, n_cols))]
    args = [x, mod_l, nw, w, b]
    out_specs = [pl.BlockSpec((1, nblk, tm, LANES), lambda bi, ti: (bi, 0, ti, 0))]
    out_shape = [jax.ShapeDtypeStruct((bsz, nblk, t, LANES), F32)]
    scratch = []
    if even:
        in_specs += [_const_spec((d, N_HEADS * LANES)), _const_spec((1, N_HEADS * LANES)),
                     _const_spec((CONV_A, 2 * GROUP_W)), _const_spec((1, 2 * GROUP_W))]
        args += [gate_w, gate_b, conv_w, conv_b]
        out_specs.append(pl.BlockSpec((1, N_HEADS, tm, LANES), lambda bi, ti: (bi, 0, ti, 0)))
        out_shape.append(jax.ShapeDtypeStruct((bsz, N_HEADS, t, LANES), F32))
        scratch.append(pltpu.VMEM((tm + SUBLANES, 2 * GROUP_W), F32))
    return pl.pallas_call(
        functools.partial(_inproj_kernel, even=even, n_cols=n_cols, tm=tm),
        grid=(bsz, t // tm),
        in_specs=in_specs,
        out_specs=out_specs,
        out_shape=out_shape,
        scratch_shapes=scratch,
        compiler_params=_cparams("arbitrary", "arbitrary"),
        name="inproj_even" if even else "inproj_odd",
    )(*args)


def _mlstm_kernel(q_ref, k_ref, v_ref, o_ref, g_ref, fb_ref, nw_ref, tri_ref, y_ref, *, seq, hb):
    L = REC_CHUNK
    row = lax.broadcasted_iota(jnp.int32, (L, L), 0)
    col = lax.broadcasted_iota(jnp.int32, (L, L), 1)
    causal = col <= row
    lane0 = lax.broadcasted_iota(jnp.int32, (L, LANES), 1) == 0
    ones = jnp.ones((L, HEAD_DIM), BF16)

    def body(c, carry):
        rows = pl.ds(pl.multiple_of(c * L, L), L)
        heads = range(hb)
        k = [k_ref[0, h, rows, :] * (HEAD_DIM ** -0.5) for h in heads]
        qb = [q_ref[0, h, rows, :].astype(BF16) for h in heads]
        v_aug = [jnp.concatenate([v_ref[0, h, rows, :].astype(BF16), ones], axis=1) for h in heads]
        gates = [g_ref[0, h, rows, :] + fb_ref[h] for h in heads]
        cum_t = [_dot_lhs01(tri_ref[...], _log_sigmoid(gates[h]), 3) for h in heads]
        s_raw = [_dot_tb(qb[h], k[h].astype(BF16)) for h in heads]
        inter = [_dot(qb[h], carry[h][0].astype(BF16)) for h in heads]
        cum, a_col, g, s_dec = [], [], [], []
        for h in heads:
            cum.append(cum_t[h][:, 1:2])
            a_col.append(gates[h][:, 0:1] - cum[h])
            a_row = jnp.where(lane0, a_col[h], 0.0).T[0:1, :]
            a_b = jnp.broadcast_to(a_row, (L, L))
            g.append(jnp.maximum(jnp.max(jnp.where(causal, a_b, NEG_BIG), axis=1, keepdims=True),
                                 carry[h][1]))
            s_dec.append((s_raw[h] * jnp.exp(jnp.where(causal, a_b - g[h], NEG_BIG))).astype(BF16))
        intra = [_dot(s_dec[h], v_aug[h]) for h in heads]
        kw, g_last = [], []
        for h in heads:
            r = intra[h] + jnp.exp(carry[h][1] - g[h]) * inter[h]
            num = r[:, :HEAD_DIM]
            den = r[:, HEAD_DIM:]
            hid = num / jnp.maximum(jnp.abs(den), jnp.exp(-(cum[h] + g[h])))
            y = _rms_lanes(hid) * nw_ref[h] * _sigmoid(o_ref[0, h, rows, :])
            y_ref[0, rows, h * HEAD_DIM:(h + 1) * HEAD_DIM] = y.astype(y_ref.dtype)
            g_last.append(g[h][L - 1:L, :])
            kw.append((k[h] * jnp.exp(a_col[h] - g_last[h])).astype(BF16))
        return tuple((jnp.exp(carry[h][1] - g_last[h]) * carry[h][0] + _dot_ta(kw[h], v_aug[h]),
                      cum[h][L - 1:L, :] + g_last[h]) for h in heads)

    init = tuple((jnp.zeros((HEAD_DIM, 2 * HEAD_DIM), F32), jnp.zeros((1, 1), F32)) for _ in range(hb))
    lax.fori_loop(0, seq // L, body, init)


def _head_spec(t, hb, off):
    assert off % hb == 0
    return pl.BlockSpec((1, hb, t, LANES), lambda bi, hi: (bi, off // hb + hi, 0, 0))


def _mixer_out(bsz, t, hb):
    return (pl.BlockSpec((1, t, hb * LANES), lambda bi, hi: (bi, 0, hi)),
            jax.ShapeDtypeStruct((bsz, t, GROUP_W), BF16))


def _mlstm(p, gates, f_bias_rows, norm_rows, tri):
    bsz, _, t, _ = p.shape
    hb = MLSTM_HEADS_PER_STEP
    out_spec, out_shape = _mixer_out(bsz, t, hb)
    per_head = pl.BlockSpec((hb, 1, LANES), lambda bi, hi: (hi, 0, 0))
    return pl.pallas_call(
        functools.partial(_mlstm_kernel, seq=t, hb=hb),
        grid=(bsz, N_HEADS // hb),
        in_specs=[_head_spec(t, hb, 0), _head_spec(t, hb, N_HEADS), _head_spec(t, hb, 2 * N_HEADS),
                  _head_spec(t, hb, 3 * N_HEADS), _head_spec(t, hb, 0), per_head, per_head,
                  _const_spec((REC_CHUNK, REC_CHUNK))],
        out_specs=out_spec,
        out_shape=out_shape,
        compiler_params=_cparams("arbitrary", "arbitrary"),
        name="mlstm",
    )(p, p, p, p, gates, f_bias_rows, norm_rows, tri)


def _hgrn_tables():
    L = REC_CHUNK
    t = np.arange(L)
    blocks = [t[None, :] <= t[:, None],
              t[None, :] > t[:, None]]
    masks = []
    lev = 0
    while (1 << lev) < L:
        h = 1 << lev
        blk = t // (2 * h)
        upper = (t % (2 * h)) >= h
        mid = blk * 2 * h + h
        eq = upper[:, None] & (t[None, :] >= mid[:, None]) & (t[None, :] <= t[:, None])
        ek = (~upper)[:, None] & (t[None, :] > t[:, None]) & (t[None, :] < mid[:, None])
        blocks.append(eq | ek)
        masks.append((blk[:, None] == blk[None, :]) & upper[:, None] & (~upper)[None, :])
        lev += 1
    masks.append(t[:, None] == t[None, :])
    w = np.concatenate(blocks, axis=0).astype(np.float32)
    m = np.stack(masks).astype(np.float32)
    return w, m, lev


def _hgrn_kernel(q_ref, f_ref, i_ref, g_ref, lbl_ref, nw_ref, w_ref, m_ref, y_ref,
                 *, seq, n_lev, layer_e, hb):
    L = REC_CHUNK

    def lower_bound(h):
        sm = jax.nn.softmax(lbl_ref[:, 0, h * HEAD_DIM:(h + 1) * HEAD_DIM], axis=0)
        lb = jnp.zeros((1, HEAD_DIM), F32)
        for r in range(1, layer_e + 1):
            lb = lb + sm[r:r + 1, :]
        return lb

    lbs = [lower_bound(h) for h in range(hb)]

    def gates(h, rows):
        lb = lbs[h]
        lb_pos = lb > 0.0
        log_lb = jnp.log(jnp.where(lb_pos, lb, 1.0))
        z = f_ref[0, h, rows, :]
        ls = jnp.log1p(-lb) + _log_sigmoid(z)
        lae = jnp.maximum(log_lb, ls) + jnp.log1p(jnp.exp(-jnp.abs(log_lb - ls)))
        return jnp.where(lb_pos, lae, ls), (1.0 - lb) * _sigmoid(-z)

    def body(c, carry):
        rows = pl.ds(pl.multiple_of(c * L, L), L)
        heads = range(hb)
        log_f, k = zip(*[gates(h, rows) for h in heads])
        dec = [jnp.exp(_dot_lhs01(w_ref[...], log_f[h], 2)) for h in heads]
        q = [_silu(q_ref[0, h, rows, :]) for h in heads]
        v = [i_ref[0, h, rows, :].astype(BF16) for h in heads]
        inter = [_dot_tb((q[h] * dec[h][0:L]).astype(BF16), carry[h].astype(BF16)) for h in heads]
        scores = []
        for h in heads:
            parts = [_dot_tb(q[h].astype(BF16), k[h].astype(BF16))]
            for lev in range(n_lev):
                fac = dec[h][(2 + lev) * L:(3 + lev) * L]
                parts.append(_dot_tb((q[h] * fac).astype(BF16), (k[h] * fac).astype(BF16)))
            scores.append(parts)
        a = []
        for h in heads:
            acc = scores[h][0] * m_ref[n_lev]
            for lev in range(n_lev):
                acc = acc + scores[h][1 + lev] * m_ref[lev]
            a.append(acc.astype(BF16))
        intra = [_dot(a[h], v[h]) for h in heads]
        for h in heads:
            y = _rms_lanes(inter[h] + intra[h]) * nw_ref[h] * _silu(g_ref[0, h, rows, :])
            y_ref[0, rows, h * HEAD_DIM:(h + 1) * HEAD_DIM] = y.astype(y_ref.dtype)
        return tuple(dec[h][L - 1:L, :] * carry[h] + _dot_ta(v[h], (k[h] * dec[h][L:2 * L]).astype(BF16))
                     for h in heads)

    lax.fori_loop(0, seq // L, body, tuple(jnp.zeros((HEAD_DIM, HEAD_DIM), F32) for _ in range(hb)))


def _hgrn(p, lb_logits_rows, norm_rows, layer_e):
    bsz, _, t, _ = p.shape
    hb = HGRN_HEADS_PER_STEP
    w_np, m_np, n_lev = _hgrn_tables()
    w_tab = jnp.asarray(w_np, BF16)
    m_tab = jnp.asarray(m_np, F32)
    n_even = lb_logits_rows.shape[0]
    out_spec, out_shape = _mixer_out(bsz, t, hb)
    return pl.pallas_call(
        functools.partial(_hgrn_kernel, seq=t, n_lev=n_lev, layer_e=layer_e, hb=hb),
        grid=(bsz, N_HEADS // hb),
        in_specs=[_head_spec(t, hb, 4 * N_HEADS), _head_spec(t, hb, 5 * N_HEADS),
                  _head_spec(t, hb, 6 * N_HEADS), _head_spec(t, hb, 7 * N_HEADS),
                  pl.BlockSpec((n_even, 1, hb * LANES), lambda bi, hi: (0, 0, hi)),
                  pl.BlockSpec((hb, 1, LANES), lambda bi, hi: (hi, 0, 0)),
                  _const_spec(w_tab.shape), _const_spec(m_tab.shape)],
        out_specs=out_spec,
        out_shape=out_shape,
        compiler_params=_cparams("arbitrary", "arbitrary"),
        name="hgrn2",
    )(p, p, p, p, lb_logits_rows, norm_rows, w_tab, m_tab)


def _retention_kernel(q_ref, k_ref, v_ref, g_ref, cos_ref, sin_ref, lg_ref, nw_ref, y_ref, *, seq, hb):
    L = REC_CHUNK
    row = lax.broadcasted_iota(jnp.int32, (L, L), 0)
    col = lax.broadcasted_iota(jnp.int32, (L, L), 1)
    dist = (row - col).astype(F32)
    pos = lax.broadcasted_iota(jnp.int32, (L, HEAD_DIM), 0).astype(F32)

    def decays(h):
        lg = lg_ref[h]
        return (jnp.where(col <= row, jnp.exp(dist * lg[:, 0:1]), 0.0),
                jnp.exp((pos + 1.0) * lg),
                jnp.exp((L - 1.0 - pos) * lg),
                jnp.exp(L * lg))

    dec = [decays(h) for h in range(hb)]

    def rot(x, cs, sn):
        return x * cs + pltpu.roll(x, HEAD_DIM // 2, 1) * sn

    def body(c, carry):
        rows = pl.ds(pl.multiple_of(c * L, L), L)
        cs = cos_ref[rows, :]
        sn = sin_ref[rows, :]
        heads = range(hb)
        q = [rot(q_ref[0, h, rows, :], cs, sn) for h in heads]
        k = [rot(k_ref[0, h, rows, :], cs, sn) * (HEAD_DIM ** -0.5) for h in heads]
        v = [v_ref[0, h, rows, :].astype(BF16) for h in heads]
        s_raw = [_dot_tb(q[h].astype(BF16), k[h].astype(BF16)) for h in heads]
        inter = [_dot_tb((q[h] * dec[h][1]).astype(BF16), carry[h].astype(BF16)) for h in heads]
        intra = [_dot((s_raw[h] * dec[h][0]).astype(BF16), v[h]) for h in heads]
        for h in heads:
            y = _rms_lanes(inter[h] + intra[h]) * nw_ref[h] * _silu(g_ref[0, h, rows, :])
            y_ref[0, rows, h * HEAD_DIM:(h + 1) * HEAD_DIM] = y.astype(y_ref.dtype)
        return tuple(dec[h][3] * carry[h] + _dot_ta(v[h], (k[h] * dec[h][2]).astype(BF16)) for h in heads)

    lax.fori_loop(0, seq // L, body, tuple(jnp.zeros((HEAD_DIM, HEAD_DIM), F32) for _ in range(hb)))


def _retention(p, norm_rows):
    bsz, _, t, _ = p.shape
    hb = RET_HEADS_PER_STEP
    inv_freq = ROPE_BASE ** (-np.arange(0, HEAD_DIM, 2, dtype=np.float32) / HEAD_DIM)
    ang = jnp.arange(t, dtype=F32)[:, None] * jnp.asarray(inv_freq, F32)[None, :]
    cos, sin = jnp.cos(ang), jnp.sin(ang)
    cos2 = jnp.concatenate([cos, cos], axis=-1)
    sin2 = jnp.concatenate([-sin, sin], axis=-1)
    log_gamma = np.log1p(-(2.0 ** (-5.0 - np.arange(N_HEADS, dtype=np.float32)))).astype(np.float32)
    lg_rows = jnp.asarray(np.broadcast_to(log_gamma[:, None, None], (N_HEADS, 1, LANES)).copy())
    out_spec, out_shape = _mixer_out(bsz, t, hb)
    per_head = pl.BlockSpec((hb, 1, LANES), lambda bi, hi: (hi, 0, 0))
    return pl.pallas_call(
        functools.partial(_retention_kernel, seq=t, hb=hb),
        grid=(bsz, N_HEADS // hb),
        in_specs=[_head_spec(t, hb, 0), _head_spec(t, hb, N_HEADS), _head_spec(t, hb, 2 * N_HEADS),
                  _head_spec(t, hb, 3 * N_HEADS),
                  _const_spec((t, LANES)), _const_spec((t, LANES)), per_head, per_head],
        out_specs=out_spec,
        out_shape=out_shape,
        compiler_params=_cparams("arbitrary", "arbitrary"),
        name="retention",
    )(p, p, p, p, cos2, sin2, lg_rows, norm_rows)


def _sb_kernel(q_ref, k_ref, v_ref, su_ref, y_ref):
    tq, tk, rs = SB_Q_TILE, SB_K_TILE, SB_STRIP
    n_strip = tq // rs
    ratio = tq // tk
    qi = pl.program_id(2)
    q2 = q_ref[0, 0] * (SB_HEAD_DIM ** -0.5)
    first = lax.broadcasted_iota(jnp.int32, (tq, LANES), 1) < SB_HEAD_DIM
    q_cat = jnp.concatenate([jnp.where(first, q2, 0.0), jnp.where(first, 0.0, q2)], axis=0).astype(BF16)
    row = lax.broadcasted_iota(jnp.int32, (rs, tk), 0)
    col = lax.broadcasted_iota(jnp.int32, (rs, tk), 1)

    def block(j, state, key_off):
        carry, acc = state
        r0 = pl.multiple_of(j * tk, tk)
        k2 = k_ref[0, 0, pl.ds(r0, tk), :].astype(BF16)
        v2 = v_ref[0, 0, pl.ds(r0, tk), :].astype(BF16)
        z = _dot_tb(q_cat, k2)

        def visible(s):
            return col + key_off < row + (s % n_strip) * rs

        lhs, keep = [], []
        for s in range(2 * n_strip):
            zs = z[s * rs:(s + 1) * rs]
            sp = jnp.maximum(zs, 0.0) + jnp.log(1.0 + jnp.exp(-jnp.abs(zs)))
            log_beta = zs - sp
            if key_off is not None:
                sp = jnp.where(visible(s), sp, 0.0)
            hi = sp.astype(BF16)
            lhs += [hi, (sp - hi.astype(F32)).astype(BF16)]
            keep.append((log_beta, sp[:, 0:1]))
        res = _dot(jnp.concatenate(lhs, axis=0), su_ref[...])
        a, new_carry = [], []
        for s in range(2 * n_strip):
            later = res[2 * s * rs:(2 * s + 1) * rs] + res[(2 * s + 1) * rs:(2 * s + 2) * rs]
            cs = carry[s * rs:(s + 1) * rs]
            arg = keep[s][0] - later - cs
            if key_off is not None:
                arg = jnp.where(visible(s), arg, NEG_BIG)
            a.append(jnp.exp(arg).astype(BF16))
            new_carry.append(cs + later[:, 0:1] + keep[s][1])
        return jnp.concatenate(new_carry, axis=0), acc + _dot(jnp.concatenate(a, axis=0), v2)

    state = (jnp.zeros((2 * tq, 1), F32), jnp.zeros((2 * tq, LANES), F32))
    for i in reversed(range(ratio)):
        state = block(qi * ratio + i, state, i * tk)
    n_full = qi * ratio
    state = lax.fori_loop(0, n_full, lambda n, st: block(n_full - 1 - n, st, None), state)
    y_ref[0] = jnp.where(first, state[1][:tq], state[1][tq:]).astype(y_ref.dtype)


def _stick_breaking(p):
    bsz, _, t, _ = p.shape
    tq, tk = SB_Q_TILE, SB_K_TILE
    n_pair = GROUP_W // LANES
    idx = np.arange(tk)
    su = jnp.asarray((idx[:, None] > idx[None, :]).astype(np.float32), BF16)
    return pl.pallas_call(
        _sb_kernel,
        grid=(bsz, n_pair, t // tq),
        in_specs=[pl.BlockSpec((1, 1, tq, LANES), lambda bi, hi, qi: (bi, 4 * N_HEADS + hi, qi, 0)),
                  pl.BlockSpec((1, 1, t, LANES), lambda bi, hi, qi: (bi, 4 * N_HEADS + n_pair + hi, 0, 0)),
                  pl.BlockSpec((1, 1, t, LANES), lambda bi, hi, qi: (bi, 4 * N_HEADS + 2 * n_pair + hi, 0, 0)),
                  _const_spec((tk, tk))],
        out_specs=pl.BlockSpec((1, tq, LANES), lambda bi, hi, qi: (bi, qi, hi)),
        out_shape=jax.ShapeDtypeStruct((bsz, t, GROUP_W), BF16),
        compiler_params=_cparams("arbitrary", "arbitrary", "arbitrary"),
        name="stick_breaking",
    )(p, p, p, su)


def _ffn_kernel(*refs, tm, final):
    if final:
        (x_ref, y1_ref, y2_ref, mod_ref, nw_ref, wo_ref, wu_ref, cw_ref, cb_ref, wd_ref, nf_ref,
         o_ref, ext_ref, carry_ref) = refs
    else:
        (x_ref, y1_ref, y2_ref, mod_ref, nw_ref, wo_ref, wu_ref, cw_ref, cb_ref, wd_ref,
         o_ref, ext_ref, carry_ref) = refs
    fc = FF_CHUNK
    n_chunks = D_FF // fc

    @pl.when(pl.program_id(1) == 0)
    def _():
        carry_ref[...] = jnp.zeros(carry_ref.shape, F32)

    mod = mod_ref[0]
    mix = _dot(y1_ref[0], wo_ref[0:GROUP_W, :]) + _dot(y2_ref[0], wo_ref[GROUP_W:, :])
    x1 = x_ref[0] + mod[2:3] * mix
    hb = _modulated_norm(x1, nw_ref[...], mod[3:4], mod[4:5]).astype(BF16)
    acc = jnp.zeros((tm, D_MODEL), F32)
    for j in range(n_chunks):
        halves = []
        for part in range(2):
            cols = slice(part * D_FF + j * fc, part * D_FF + (j + 1) * fc)
            ext_ref[0:SUBLANES, :] = carry_ref[2 * j + part]
            ext_ref[SUBLANES:SUBLANES + tm, :] = _dot(hb, wu_ref[:, cols])
            carry_ref[2 * j + part] = ext_ref[tm:tm + SUBLANES, :]
            u = cb_ref[:, cols]
            for i in range(FFN_CONV):
                off = SUBLANES - (FFN_CONV - 1) + i
                u = u + cw_ref[i:i + 1, cols] * ext_ref[off:off + tm, :]
            halves.append(u)
        act = (halves[0] * _silu(halves[1])).astype(BF16)
        acc = acc + _dot(act, wd_ref[j * fc:(j + 1) * fc, :])
    x2 = x1 + mod[5:6] * acc
    if final:
        x2 = _rms_lanes(x2) * nf_ref[...]
    o_ref[0] = x2


def _ffn(x, y1, y2, mod_l, nw, w_out, w_up, conv_w, conv_b, w_down, norm_final=None):
    bsz, t, d = x.shape
    tm = min(TOK_TILE, t)
    final = norm_final is not None
    in_specs = [pl.BlockSpec((1, tm, d), lambda bi, ti: (bi, ti, 0)),
                pl.BlockSpec((1, tm, GROUP_W), lambda bi, ti: (bi, ti, 0)),
                pl.BlockSpec((1, tm, GROUP_W), lambda bi, ti: (bi, ti, 0)),
                pl.BlockSpec((1, 6, d), lambda bi, ti: (bi, 0, 0)),
                _const_spec((1, d)),
                _const_spec(w_out.shape), _const_spec(w_up.shape),
                _const_spec(conv_w.shape), _const_spec(conv_b.shape),
                _const_spec(w_down.shape)]
    args = [x, y1, y2, mod_l, nw, w_out, w_up, conv_w, conv_b, w_down]
    if final:
        in_specs.append(_const_spec((1, d)))
        args.append(norm_final)
    return pl.pallas_call(
        functools.partial(_ffn_kernel, tm=tm, final=final),
        grid=(bsz, t // tm),
        in_specs=in_specs,
        out_specs=pl.BlockSpec((1, tm, d), lambda bi, ti: (bi, ti, 0)),
        out_shape=jax.ShapeDtypeStruct((bsz, t, d), F32),
        scratch_shapes=[pltpu.VMEM((tm + SUBLANES, FF_CHUNK), F32),
                        pltpu.VMEM((2 * (D_FF // FF_CHUNK), SUBLANES, FF_CHUNK), F32)],
        compiler_params=_cparams("arbitrary", "arbitrary"),
        name="outproj_ffn_final" if final else "outproj_ffn",
    )(*args)


def _head_rows(vec):
    return vec.reshape(-1, 1, LANES)


def kernel(x, c, w_ada, b_ada, norm_mix, norm_ffn, norm_final, w_in_even, b_in_even, conv_qk_w,
           conv_qk_b, mlstm_f_bias, mlstm_norm, hgrn_lb_logits, hgrn_norm, w_in_odd, b_in_odd,
           ret_norm, w_out, ffn_up, ffn_conv_w, ffn_conv_b, ffn_down):
    bsz, t, d = x.shape
    assert d == D_MODEL and t % SB_Q_TILE == 0 and t % REC_CHUNK == 0
    mod = _ada(c, w_ada, b_ada).reshape(DEPTH, bsz, 6, d)
    idx = np.arange(REC_CHUNK)
    tri = jnp.asarray((idx[None, :] <= idx[:, None]).astype(np.float32), BF16)
    gate0 = 4 * GROUP_W
    n_even = hgrn_lb_logits.shape[0]
    lb_rows = hgrn_lb_logits.astype(F32).reshape(n_even, 1, GROUP_W)

    for layer in range(DEPTH):
        mod_l = mod[layer]
        nw_mix = norm_mix[layer][None, :]
        if layer % 2 == 0:
            e = layer // 2
            w, b = w_in_even[e], b_in_even[e]
            w_main = jnp.concatenate([w[:, :gate0], w[:, gate0 + 2 * N_HEADS:]], axis=1).astype(BF16)
            b_main = jnp.concatenate([b[:gate0], b[gate0 + 2 * N_HEADS:]])[None, :]
            gw = jnp.zeros((d, N_HEADS, LANES), F32)
            gw = gw.at[:, :, 0].set(w[:, gate0:gate0 + N_HEADS])
            gw = gw.at[:, :, 1].set(w[:, gate0 + N_HEADS:gate0 + 2 * N_HEADS])
            gb = jnp.zeros((N_HEADS, LANES), F32)
            gb = gb.at[:, 0].set(b[gate0:gate0 + N_HEADS]).at[:, 1].set(b[gate0 + N_HEADS:gate0 + 2 * N_HEADS])
            p, gates = _inproj(x, mod_l, nw_mix, w_main, b_main, True,
                               gw.reshape(d, N_HEADS * LANES).astype(BF16), gb.reshape(1, -1),
                               conv_qk_w[e], conv_qk_b[e][None, :])
            fb_rows = jnp.zeros((N_HEADS, 1, LANES), F32).at[:, 0, 1].set(mlstm_f_bias[e])
            y1 = _mlstm(p, gates, fb_rows, _head_rows(mlstm_norm[e]), tri)
            y2 = _hgrn(p, lb_rows, _head_rows(hgrn_norm[e]), e)
        else:
            o = layer // 2
            (p,) = _inproj(x, mod_l, nw_mix, w_in_odd[o].astype(BF16), b_in_odd[o][None, :], False)
            y1 = _retention(p, _head_rows(ret_norm[o]))
            y2 = _stick_breaking(p)
        x = _ffn(x, y1, y2, mod_l, norm_ffn[layer][None, :], w_out[layer].astype(BF16),
                 ffn_up[layer].astype(BF16), ffn_conv_w[layer], ffn_conv_b[layer][None, :],
                 ffn_down[layer].astype(BF16),
                 norm_final[None, :] if layer == DEPTH - 1 else None)
    return x
```

```python
import functools

import numpy as np
import jax
import jax.numpy as jnp
from jax import lax
from jax.experimental import pallas as pl
from jax.experimental.pallas import tpu as pltpu

F32 = jnp.float32
BF16 = jnp.bfloat16

D_MODEL = 1024
DEPTH = 4
HEAD_DIM = 128
GROUP_W = D_MODEL // 2
N_HEADS = GROUP_W // HEAD_DIM
SB_HEAD_DIM = 64
CONV_A = 4
FFN_CONV = 3
D_FF = 2816
ROPE_BASE = 10000.0
EPS = 1e-6

LANES = 128
SUBLANES = 8
REC_CHUNK = 128
MLSTM_HEADS_PER_STEP = 4
HGRN_HEADS_PER_STEP = 2
RET_HEADS_PER_STEP = 4
SB_Q_TILE = 512
SB_K_TILE = 256
SB_STRIP = 16
TOK_TILE = 512
FF_CHUNK = 256
FF_DOWN_GROUP = 4
VMEM_LIMIT = 60 * 1024 * 1024
NEG_BIG = -1e30
LOG2_E = 1.4426950408889634


def _dot(a, b):
    return jnp.dot(a, b, preferred_element_type=F32)


def _dot_tb(a, b):
    return lax.dot_general(a, b, (((1,), (1,)), ((), ())), preferred_element_type=F32)


def _dot_ta(a, b):
    return lax.dot_general(a, b, (((0,), (0,)), ((), ())), preferred_element_type=F32)


def _split3(x):
    hi = x.astype(BF16)
    r1 = x - hi.astype(F32)
    mid = r1.astype(BF16)
    lo = (r1 - mid.astype(F32)).astype(BF16)
    return hi, mid, lo


def _dot_lhs01(w01, x, terms):
    parts = _split3(x)[:terms]
    out = _dot(w01, parts[0])
    for part in parts[1:]:
        out = out + _dot(w01, part)
    return out


def _sigmoid(x):
    return jax.nn.sigmoid(x)


def _silu(x):
    return x * jax.nn.sigmoid(x)


def _log_sigmoid(x):
    return jnp.minimum(x, 0.0) - jnp.log1p(jnp.exp(-jnp.abs(x)))


def _rms_lanes(x):
    return x * lax.rsqrt(jnp.mean(x * x, axis=-1, keepdims=True) + EPS)


def _cparams(*sem):
    return pltpu.CompilerParams(dimension_semantics=sem, vmem_limit_bytes=VMEM_LIMIT)


def _const_spec(shape):
    nd = len(shape)
    return pl.BlockSpec(shape, lambda *_: (0,) * nd, pipeline_mode=pl.Buffered(1))


def _ada_kernel(c_ref, w_ref, b_ref, o_ref):
    c = c_ref[...]
    ada = jnp.dot(_silu(c), w_ref[...], preferred_element_type=F32,
                  precision=lax.Precision.HIGHEST)
    for layer in range(DEPTH):
        o_ref[layer] = ada + b_ref[layer:layer + 1, :]


def _ada(c, w_ada, b_ada):
    bsz, d = c.shape
    n = w_ada.shape[1]
    tn = 1024
    return pl.pallas_call(
        _ada_kernel,
        grid=(n // tn,),
        in_specs=[pl.BlockSpec((bsz, d), lambda j: (0, 0)),
                  pl.BlockSpec((d, tn), lambda j: (0, j)),
                  pl.BlockSpec((DEPTH, tn), lambda j: (0, j))],
        out_specs=pl.BlockSpec((DEPTH, bsz, tn), lambda j: (0, 0, j)),
        out_shape=jax.ShapeDtypeStruct((DEPTH, bsz, n), F32),
        compiler_params=_cparams("arbitrary"),
        name="ada_mod",
    )(c, w_ada, b_ada)


def _modulated_norm(x, nw, shift, scale):
    return _rms_lanes(x) * nw * (1.0 + scale) + shift


def _inproj_kernel(*refs, even, n_cols, tm):
    if even:
        (x_ref, mod_ref, nw_ref, w_ref, b_ref, wg_ref, bg_ref, cw_ref, cb_ref,
         p_ref, g_ref, ext_ref) = refs
    else:
        x_ref, mod_ref, nw_ref, w_ref, b_ref, p_ref = refs
    mod = mod_ref[0]
    h = _modulated_norm(x_ref[0], nw_ref[...], mod[0:1], mod[1:2])
    hb = h.astype(BF16)
    col_chunk = 4 * LANES
    conv_cols = 2 * GROUP_W

    if even:
        @pl.when(pl.program_id(1) == 0)
        def _():
            ext_ref[0:SUBLANES, :] = jnp.zeros((SUBLANES, conv_cols), F32)

    def project(j):
        cols = slice(j * col_chunk, (j + 1) * col_chunk)
        return _dot(hb, w_ref[:, cols]) + b_ref[:, cols]

    n_chunks = n_cols // col_chunk
    ahead = project(0)
    for j in range(n_chunks):
        cols = slice(j * col_chunk, (j + 1) * col_chunk)
        res = ahead
        if j + 1 < n_chunks:
            ahead = project(j + 1)
        if even and (j + 1) * col_chunk <= conv_cols:
            ext_ref[SUBLANES:SUBLANES + tm, cols] = res
            acc = cb_ref[:, cols]
            for i in range(CONV_A):
                off = SUBLANES - (CONV_A - 1) + i
                acc = acc + cw_ref[i:i + 1, cols] * ext_ref[off:off + tm, cols]
            ext_ref[0:SUBLANES, cols] = ext_ref[tm:tm + SUBLANES, cols]
            res = _silu(acc)
        for i in range(col_chunk // LANES):
            p_ref[0, j * (col_chunk // LANES) + i] = res[:, i * LANES:(i + 1) * LANES]

    if even:
        g_ref[0] = _dot(hb, wg_ref[...]) + bg_ref[...]


def _inproj(x, mod_l, nw, w, b, even, gate_w=None, gate_b=None, conv_w=None, conv_b=None):
    bsz, t, d = x.shape
    n_cols = w.shape[1]
    nblk = n_cols // LANES
    tm = min(TOK_TILE, t)
    in_specs = [pl.BlockSpec((1, tm, d), lambda bi, ti: (bi, ti, 0)),
                pl.BlockSpec((1, 6, d), lambda bi, ti: (bi, 0, 0)),
                _const_spec((1, d)),
                _const_spec((d, n_cols)),
                _const_spec((1, n_cols))]
    args = [x, mod_l, nw, w, b]
    out_specs = [pl.BlockSpec((1, nblk, tm, LANES), lambda bi, ti: (bi, 0, ti, 0))]
    out_shape = [jax.ShapeDtypeStruct((bsz, nblk, t, LANES), F32)]
    scratch = []
    if even:
        in_specs += [_const_spec((d, LANES)), _const_spec((1, LANES)),
                     _const_spec((CONV_A, 2 * GROUP_W)), _const_spec((1, 2 * GROUP_W))]
        args += [gate_w, gate_b, conv_w, conv_b]
        out_specs.append(pl.BlockSpec((1, tm, LANES), lambda bi, ti: (bi, ti, 0)))
        out_shape.append(jax.ShapeDtypeStruct((bsz, t, LANES), F32))
        scratch.append(pltpu.VMEM((tm + SUBLANES, 2 * GROUP_W), F32))
    return pl.pallas_call(
        functools.partial(_inproj_kernel, even=even, n_cols=n_cols, tm=tm),
        grid=(bsz, t // tm),
        in_specs=in_specs,
        out_specs=out_specs,
        out_shape=out_shape,
        scratch_shapes=scratch,
        compiler_params=_cparams("arbitrary", "arbitrary"),
        name="inproj_even" if even else "inproj_odd",
    )(*args)


def _mlstm_kernel(q_ref, k_ref, v_ref, o_ref, g_ref, fb_ref, nw_ref, tri_ref, y_ref, *, seq, hb):
    L = REC_CHUNK
    row = lax.broadcasted_iota(jnp.int32, (L, L), 0)
    col = lax.broadcasted_iota(jnp.int32, (L, L), 1)
    causal = col <= row
    lane0 = lax.broadcasted_iota(jnp.int32, (L, LANES), 1) == 0
    ones = jnp.ones((L, HEAD_DIM), BF16)

    def body(c, carry):
        rows = pl.ds(pl.multiple_of(c * L, L), L)
        heads = range(hb)
        k = [k_ref[0, h, rows, :] * (HEAD_DIM ** -0.5) for h in heads]
        qb = [q_ref[0, h, rows, :].astype(BF16) for h in heads]
        v_aug = [jnp.concatenate([v_ref[0, h, rows, :].astype(BF16), ones], axis=1) for h in heads]
        gates = g_ref[0, rows, :] + fb_ref[...]
        cum_t = _dot_lhs01(tri_ref[...], _log_sigmoid(gates), 3)
        s_raw = [_dot_tb(qb[h], k[h].astype(BF16)) for h in heads]
        inter = [_dot(qb[h], carry[h][0].astype(BF16)) for h in heads]
        cum, a_col, g, s_dec = [], [], [], []
        for h in heads:
            cum.append(cum_t[:, N_HEADS + h:N_HEADS + h + 1])
            a_col.append(gates[:, h:h + 1] - cum[h])
            a_row = jnp.where(lane0, a_col[h], 0.0).T[0:1, :]
            a_b = jnp.broadcast_to(a_row, (L, L))
            g.append(jnp.maximum(jnp.max(jnp.where(causal, a_b, NEG_BIG), axis=1, keepdims=True),
                                 carry[h][1]))
            s_dec.append((s_raw[h] * jnp.exp(jnp.where(causal, a_b - g[h], NEG_BIG))).astype(BF16))
        intra = [_dot(s_dec[h], v_aug[h]) for h in heads]
        kw, g_last = [], []
        for h in heads:
            r = intra[h] + jnp.exp(carry[h][1] - g[h]) * inter[h]
            num = r[:, :HEAD_DIM]
            den = r[:, HEAD_DIM:]
            hid = num / jnp.maximum(jnp.abs(den), jnp.exp(-(cum[h] + g[h])))
            y = _rms_lanes(hid) * nw_ref[h] * _sigmoid(o_ref[0, h, rows, :])
            y_ref[0, rows, h * HEAD_DIM:(h + 1) * HEAD_DIM] = y.astype(y_ref.dtype)
            g_last.append(g[h][L - 1:L, :])
            kw.append((k[h] * jnp.exp(a_col[h] - g_last[h])).astype(BF16))
        return tuple((jnp.exp(carry[h][1] - g_last[h]) * carry[h][0] + _dot_ta(kw[h], v_aug[h]),
                      cum[h][L - 1:L, :] + g_last[h]) for h in heads)

    init = tuple((jnp.zeros((HEAD_DIM, 2 * HEAD_DIM), F32), jnp.zeros((1, 1), F32)) for _ in range(hb))
    lax.fori_loop(0, seq // L, body, init)


def _head_spec(t, hb, off):
    assert off % hb == 0
    return pl.BlockSpec((1, hb, t, LANES), lambda bi, hi: (bi, off // hb + hi, 0, 0))


def _mixer_out(bsz, t, hb):
    return (pl.BlockSpec((1, t, hb * LANES), lambda bi, hi: (bi, 0, hi)),
            jax.ShapeDtypeStruct((bsz, t, GROUP_W), BF16))


def _mlstm(p, gates, f_bias_row, norm_rows, tri):
    bsz, _, t, _ = p.shape
    hb = MLSTM_HEADS_PER_STEP
    assert hb == N_HEADS
    out_spec, out_shape = _mixer_out(bsz, t, hb)
    return pl.pallas_call(
        functools.partial(_mlstm_kernel, seq=t, hb=hb),
        grid=(bsz, N_HEADS // hb),
        in_specs=[_head_spec(t, hb, 0), _head_spec(t, hb, N_HEADS), _head_spec(t, hb, 2 * N_HEADS),
                  _head_spec(t, hb, 3 * N_HEADS),
                  pl.BlockSpec((1, t, LANES), lambda bi, hi: (bi, 0, 0)),
                  _const_spec((1, LANES)),
                  pl.BlockSpec((hb, 1, LANES), lambda bi, hi: (hi, 0, 0)),
                  _const_spec((REC_CHUNK, REC_CHUNK))],
        out_specs=out_spec,
        out_shape=out_shape,
        compiler_params=_cparams("arbitrary", "arbitrary"),
        name="mlstm",
    )(p, p, p, p, gates, f_bias_row, norm_rows, tri)


def _hgrn_tables():
    L = REC_CHUNK
    t = np.arange(L)
    blocks = [t[None, :] <= t[:, None],
              t[None, :] > t[:, None]]
    masks = []
    lev = 0
    while (1 << lev) < L:
        h = 1 << lev
        blk = t // (2 * h)
        upper = (t % (2 * h)) >= h
        mid = blk * 2 * h + h
        eq = upper[:, None] & (t[None, :] >= mid[:, None]) & (t[None, :] <= t[:, None])
        ek = (~upper)[:, None] & (t[None, :] > t[:, None]) & (t[None, :] < mid[:, None])
        blocks.append(eq | ek)
        masks.append((blk[:, None] == blk[None, :]) & upper[:, None] & (~upper)[None, :])
        lev += 1
    masks.append(t[:, None] == t[None, :])
    w = np.concatenate(blocks, axis=0).astype(np.float32)
    m = np.stack(masks).astype(np.float32)
    return w, m, lev


def _hgrn_kernel(q_ref, f_ref, i_ref, g_ref, lbl_ref, nw_ref, w_ref, m_ref, y_ref,
                 *, seq, n_lev, layer_e, hb):
    L = REC_CHUNK

    def lower_bound(h):
        sm = jax.nn.softmax(lbl_ref[:, 0, h * HEAD_DIM:(h + 1) * HEAD_DIM], axis=0)
        lb = jnp.zeros((1, HEAD_DIM), F32)
        for r in range(1, layer_e + 1):
            lb = lb + sm[r:r + 1, :]
        return lb

    lbs = [lower_bound(h) for h in range(hb)]

    def gates(h, rows):
        lb = lbs[h]
        lb_pos = lb > 0.0
        log_lb = jnp.log(jnp.where(lb_pos, lb, 1.0))
        z = f_ref[0, h, rows, :]
        ls = jnp.log1p(-lb) + _log_sigmoid(z)
        lae = jnp.maximum(log_lb, ls) + jnp.log1p(jnp.exp(-jnp.abs(log_lb - ls)))
        return jnp.where(lb_pos, lae, ls), (1.0 - lb) * _sigmoid(-z)

    def body(c, carry):
        rows = pl.ds(pl.multiple_of(c * L, L), L)
        heads = range(hb)
        log_f, k = zip(*[gates(h, rows) for h in heads])
        dec = [jnp.exp(_dot_lhs01(w_ref[...], log_f[h], 2)) for h in heads]
        q = [_silu(q_ref[0, h, rows, :]) for h in heads]
        v = [i_ref[0, h, rows, :].astype(BF16) for h in heads]
        inter = [_dot_tb((q[h] * dec[h][0:L]).astype(BF16), carry[h].astype(BF16)) for h in heads]
        scores = []
        for h in heads:
            parts = [_dot_tb(q[h].astype(BF16), k[h].astype(BF16))]
            for lev in range(n_lev):
                fac = dec[h][(2 + lev) * L:(3 + lev) * L]
                parts.append(_dot_tb((q[h] * fac).astype(BF16), (k[h] * fac).astype(BF16)))
            scores.append(parts)
        a = []
        for h in heads:
            acc = scores[h][0] * m_ref[n_lev]
            for lev in range(n_lev):
                acc = acc + scores[h][1 + lev] * m_ref[lev]
            a.append(acc.astype(BF16))
        intra = [_dot(a[h], v[h]) for h in heads]
        for h in heads:
            y = _rms_lanes(inter[h] + intra[h]) * nw_ref[h] * _silu(g_ref[0, h, rows, :])
            y_ref[0, rows, h * HEAD_DIM:(h + 1) * HEAD_DIM] = y.astype(y_ref.dtype)
        return tuple(dec[h][L - 1:L, :] * carry[h] + _dot_ta(v[h], (k[h] * dec[h][L:2 * L]).astype(BF16))
                     for h in heads)

    lax.fori_loop(0, seq // L, body, tuple(jnp.zeros((HEAD_DIM, HEAD_DIM), F32) for _ in range(hb)))


def _hgrn(p, lb_logits_rows, norm_rows, layer_e):
    bsz, _, t, _ = p.shape
    hb = HGRN_HEADS_PER_STEP
    w_np, m_np, n_lev = _hgrn_tables()
    w_tab = jnp.asarray(w_np, BF16)
    m_tab = jnp.asarray(m_np, F32)
    n_even = lb_logits_rows.shape[0]
    out_spec, out_shape = _mixer_out(bsz, t, hb)
    return pl.pallas_call(
        functools.partial(_hgrn_kernel, seq=t, n_lev=n_lev, layer_e=layer_e, hb=hb),
        grid=(bsz, N_HEADS // hb),
        in_specs=[_head_spec(t, hb, 4 * N_HEADS), _head_spec(t, hb, 5 * N_HEADS),
                  _head_spec(t, hb, 6 * N_HEADS), _head_spec(t, hb, 7 * N_HEADS),
                  pl.BlockSpec((n_even, 1, hb * LANES), lambda bi, hi: (0, 0, hi)),
                  pl.BlockSpec((hb, 1, LANES), lambda bi, hi: (hi, 0, 0)),
                  _const_spec(w_tab.shape), _const_spec(m_tab.shape)],
        out_specs=out_spec,
        out_shape=out_shape,
        compiler_params=_cparams("arbitrary", "arbitrary"),
        name="hgrn2",
    )(p, p, p, p, lb_logits_rows, norm_rows, w_tab, m_tab)


def _retention_kernel(q_ref, k_ref, v_ref, g_ref, cos_ref, sin_ref, lg_ref, nw_ref, y_ref, *, seq, hb):
    L = REC_CHUNK
    row = lax.broadcasted_iota(jnp.int32, (L, L), 0)
    col = lax.broadcasted_iota(jnp.int32, (L, L), 1)
    dist = (row - col).astype(F32)
    pos = lax.broadcasted_iota(jnp.int32, (L, HEAD_DIM), 0).astype(F32)

    def decays(h):
        lg = lg_ref[h]
        return (jnp.where(col <= row, jnp.exp(dist * lg[:, 0:1]), 0.0),
                jnp.exp((pos + 1.0) * lg),
                jnp.exp((L - 1.0 - pos) * lg),
                jnp.exp(L * lg))

    dec = [decays(h) for h in range(hb)]

    def rot(x, cs, sn):
        return x * cs + pltpu.roll(x, HEAD_DIM // 2, 1) * sn

    def body(c, carry):
        rows = pl.ds(pl.multiple_of(c * L, L), L)
        cs = cos_ref[rows, :]
        sn = sin_ref[rows, :]
        heads = range(hb)
        q = [rot(q_ref[0, h, rows, :], cs, sn) for h in heads]
        k = [rot(k_ref[0, h, rows, :], cs, sn) * (HEAD_DIM ** -0.5) for h in heads]
        v = [v_ref[0, h, rows, :].astype(BF16) for h in heads]
        s_raw = [_dot_tb(q[h].astype(BF16), k[h].astype(BF16)) for h in heads]
        inter = [_dot_tb((q[h] * dec[h][1]).astype(BF16), carry[h].astype(BF16)) for h in heads]
        intra = [_dot((s_raw[h] * dec[h][0]).astype(BF16), v[h]) for h in heads]
        for h in heads:
            y = _rms_lanes(inter[h] + intra[h]) * nw_ref[h] * _silu(g_ref[0, h, rows, :])
            y_ref[0, rows, h * HEAD_DIM:(h + 1) * HEAD_DIM] = y.astype(y_ref.dtype)
        return tuple(dec[h][3] * carry[h] + _dot_ta(v[h], (k[h] * dec[h][2]).astype(BF16)) for h in heads)

    lax.fori_loop(0, seq // L, body, tuple(jnp.zeros((HEAD_DIM, HEAD_DIM), F32) for _ in range(hb)))


def _retention(p, norm_rows):
    bsz, _, t, _ = p.shape
    hb = RET_HEADS_PER_STEP
    inv_freq = ROPE_BASE ** (-np.arange(0, HEAD_DIM, 2, dtype=np.float32) / HEAD_DIM)
    ang = jnp.arange(t, dtype=F32)[:, None] * jnp.asarray(inv_freq, F32)[None, :]
    cos, sin = jnp.cos(ang), jnp.sin(ang)
    cos2 = jnp.concatenate([cos, cos], axis=-1)
    sin2 = jnp.concatenate([-sin, sin], axis=-1)
    log_gamma = np.log1p(-(2.0 ** (-5.0 - np.arange(N_HEADS, dtype=np.float32)))).astype(np.float32)
    lg_rows = jnp.asarray(np.broadcast_to(log_gamma[:, None, None], (N_HEADS, 1, LANES)).copy())
    out_spec, out_shape = _mixer_out(bsz, t, hb)
    per_head = pl.BlockSpec((hb, 1, LANES), lambda bi, hi: (hi, 0, 0))
    return pl.pallas_call(
        functools.partial(_retention_kernel, seq=t, hb=hb),
        grid=(bsz, N_HEADS // hb),
        in_specs=[_head_spec(t, hb, 0), _head_spec(t, hb, N_HEADS), _head_spec(t, hb, 2 * N_HEADS),
                  _head_spec(t, hb, 3 * N_HEADS),
                  _const_spec((t, LANES)), _const_spec((t, LANES)), per_head, per_head],
        out_specs=out_spec,
        out_shape=out_shape,
        compiler_params=_cparams("arbitrary", "arbitrary"),
        name="retention",
    )(p, p, p, p, cos2, sin2, lg_rows, norm_rows)


def _sb_kernel(q_ref, k_ref, v_ref, su_ref, y_ref, z0_ref, z1_ref, lhs0_ref, lhs1_ref, w0_ref, w1_ref,
               sum0_ref, sum1_ref, carry_ref, acc_ref):
    z_bufs, lhs_bufs = (z0_ref, z1_ref), (lhs0_ref, lhs1_ref)
    w_bufs, sum_bufs = (w0_ref, w1_ref), (sum0_ref, sum1_ref)
    tq, tk, rs = SB_Q_TILE, SB_K_TILE, SB_STRIP
    n_strip = tq // rs
    ratio = tq // tk
    assert ratio == 2
    qi = pl.program_id(2)
    n_full = qi * ratio
    q2 = q_ref[0, 0] * (SB_HEAD_DIM ** -0.5 * LOG2_E)
    first = lax.broadcasted_iota(jnp.int32, (tq, LANES), 1) < SB_HEAD_DIM
    q_cat = jnp.concatenate([jnp.where(first, q2, 0.0), jnp.where(first, 0.0, q2)], axis=0).astype(BF16)
    row = lax.broadcasted_iota(jnp.int32, (rs, tk), 0)
    col = lax.broadcasted_iota(jnp.int32, (rs, tk), 1)

    def tile_of(idx):
        if isinstance(idx, int) and idx < ratio:
            return qi * ratio + (ratio - 1 - idx), (ratio - 1 - idx) * tk
        return n_full - 1 - (idx - ratio), None

    def rows_of(idx):
        return pl.ds(pl.multiple_of(tile_of(idx)[0] * tk, tk), tk)

    def visible(s, key_off):
        return col + key_off < row + (s % n_strip) * rs

    def stage_a(idx, slot):
        z_bufs[slot][...] = _dot_tb(q_cat, k_ref[0, 0, rows_of(idx), :].astype(BF16))

    def stage_b(idx, slot):
        key_off = tile_of(idx)[1]
        z_buf, lhs_buf, w_buf = z_bufs[slot], lhs_bufs[slot], w_bufs[slot]
        for s in range(2 * n_strip):
            strip = slice(s * rs, (s + 1) * rs)
            zs = z_buf[strip, :]
            sp = jnp.maximum(zs, 0.0) + jnp.log2(1.0 + jnp.exp2(-jnp.abs(zs)))
            if key_off is not None:
                sp = jnp.where(visible(s, key_off), sp, 0.0)
            hi = sp.astype(BF16)
            lhs_buf[strip, 0:tk] = hi
            lhs_buf[strip, tk:2 * tk] = (sp - hi.astype(F32)).astype(BF16)
        incl = _dot(lhs_buf[...], su_ref[...])
        for s in range(2 * n_strip):
            strip = slice(s * rs, (s + 1) * rs)
            arg = z_buf[strip, :] - incl[strip]
            if key_off is not None:
                arg = jnp.where(visible(s, key_off), arg, NEG_BIG)
            w_buf[strip, :] = jnp.exp2(arg).astype(BF16)
        sum_bufs[slot][...] = jnp.broadcast_to(incl[:, 0:1], (2 * tq, LANES))

    def stage_c(idx, slot):
        v2 = v_ref[0, 0, rows_of(idx), :].astype(BF16)
        carry = carry_ref[...]
        acc_ref[...] += jnp.exp2(-carry) * _dot(w_bufs[slot][...], v2)
        carry_ref[...] = carry + sum_bufs[slot][...]

    def finish():
        y_ref[0] = jnp.where(first, acc_ref[0:tq, :], acc_ref[tq:2 * tq, :]).astype(y_ref.dtype)

    carry_ref[...] = jnp.zeros((2 * tq, LANES), F32)
    acc_ref[...] = jnp.zeros((2 * tq, LANES), F32)
    stage_a(0, 0)
    stage_a(1, 1)
    stage_b(0, 0)

    @pl.when(qi == 0)
    def _():
        stage_b(1, 1)
        stage_c(0, 0)
        stage_c(1, 1)
        finish()

    @pl.when(qi > 0)
    def _():
        stage_a(2, 0)
        stage_b(1, 1)
        stage_c(0, 0)
        stage_a(3, 1)
        stage_b(2, 0)
        stage_c(1, 1)

        def two_steps(i, _):
            n = 2 + 2 * i
            stage_a(n + 2, 0)
            stage_b(n + 1, 1)
            stage_c(n, 0)
            stage_a(n + 3, 1)
            stage_b(n + 2, 0)
            stage_c(n + 1, 1)
            return 0

        lax.fori_loop(0, qi - 1, two_steps, 0)
        last = ratio + n_full - 1
        stage_b(last, 1)
        stage_c(last - 1, 0)
        stage_c(last, 1)
        finish()


def _stick_breaking(p):
    bsz, _, t, _ = p.shape
    tq, tk = SB_Q_TILE, SB_K_TILE
    n_pair = GROUP_W // LANES
    idx = np.arange(tk)
    from_s = (idx[:, None] >= idx[None, :]).astype(np.float32)
    su = jnp.asarray(np.concatenate([from_s, from_s], axis=0), BF16)
    return pl.pallas_call(
        _sb_kernel,
        grid=(bsz, n_pair, t // tq),
        in_specs=[pl.BlockSpec((1, 1, tq, LANES), lambda bi, hi, qi: (bi, 4 * N_HEADS + hi, qi, 0)),
                  pl.BlockSpec((1, 1, t, LANES), lambda bi, hi, qi: (bi, 4 * N_HEADS + n_pair + hi, 0, 0)),
                  pl.BlockSpec((1, 1, t, LANES), lambda bi, hi, qi: (bi, 4 * N_HEADS + 2 * n_pair + hi, 0, 0)),
                  _const_spec((2 * tk, tk))],
        out_specs=pl.BlockSpec((1, tq, LANES), lambda bi, hi, qi: (bi, qi, hi)),
        out_shape=jax.ShapeDtypeStruct((bsz, t, GROUP_W), BF16),
        scratch_shapes=(2 * [pltpu.VMEM((2 * tq, tk), F32)] + 2 * [pltpu.VMEM((2 * tq, 2 * tk), BF16)]
                        + 2 * [pltpu.VMEM((2 * tq, tk), BF16)] + 4 * [pltpu.VMEM((2 * tq, LANES), F32)]),
        compiler_params=_cparams("arbitrary", "arbitrary", "arbitrary"),
        name="stick_breaking",
    )(p, p, p, su)


def _ffn_kernel(*refs, tm, final):
    if final:
        (x_ref, y1_ref, y2_ref, mod_ref, nw_ref, wo_ref, wu_ref, cw_ref, cb_ref, wd_ref, nf_ref,
         o_ref, ext_ref, carry_ref, act_ref) = refs
    else:
        (x_ref, y1_ref, y2_ref, mod_ref, nw_ref, wo_ref, wu_ref, cw_ref, cb_ref, wd_ref,
         o_ref, ext_ref, carry_ref, act_ref) = refs
    fc = FF_CHUNK
    n_chunks = D_FF // fc

    @pl.when(pl.program_id(1) == 0)
    def _():
        carry_ref[...] = jnp.zeros(carry_ref.shape, F32)

    mod = mod_ref[0]
    mix = _dot(y1_ref[0], wo_ref[0:GROUP_W, :]) + _dot(y2_ref[0], wo_ref[GROUP_W:, :])
    x1 = x_ref[0] + mod[2:3] * mix
    hb = _modulated_norm(x1, nw_ref[...], mod[3:4], mod[4:5]).astype(BF16)
    acc = jnp.zeros((tm, D_MODEL), F32)

    def cols_of(j, part):
        return slice(part * D_FF + j * fc, part * D_FF + (j + 1) * fc)

    def up_project(j):
        for part in range(2):
            ext = ext_ref.at[2 * (j % 2) + part]
            ext[0:SUBLANES, :] = carry_ref[2 * j + part]
            ext[SUBLANES:SUBLANES + tm, :] = _dot(hb, wu_ref[:, cols_of(j, part)])
            carry_ref[2 * j + part] = ext[tm:tm + SUBLANES, :]

    def down_project(acc, ks):
        return acc + _dot(act_ref[:, ks], wd_ref[ks, :])

    up_project(0)
    pending = None
    for j in range(n_chunks):
        if j + 1 < n_chunks:
            up_project(j + 1)
        if pending is not None:
            acc = down_project(acc, pending)
            pending = None
        halves = []
        for part in range(2):
            cols = cols_of(j, part)
            ext = ext_ref.at[2 * (j % 2) + part]
            u = cb_ref[:, cols]
            for i in range(FFN_CONV):
                off = SUBLANES - (FFN_CONV - 1) + i
                u = u + cw_ref[i:i + 1, cols] * ext[off:off + tm, :]
            halves.append(u)
        act_ref[:, j * fc:(j + 1) * fc] = (halves[0] * _silu(halves[1])).astype(BF16)
        if (j + 1) % FF_DOWN_GROUP == 0 or j + 1 == n_chunks:
            pending = slice((j // FF_DOWN_GROUP) * FF_DOWN_GROUP * fc, (j + 1) * fc)
    x2 = x1 + mod[5:6] * down_project(acc, pending)
    if final:
        x2 = _rms_lanes(x2) * nf_ref[...]
    o_ref[0] = x2


def _ffn(x, y1, y2, mod_l, nw, w_out, w_up, conv_w, conv_b, w_down, norm_final=None):
    bsz, t, d = x.shape
    tm = min(TOK_TILE, t)
    final = norm_final is not None
    in_specs = [pl.BlockSpec((1, tm, d), lambda bi, ti: (bi, ti, 0)),
                pl.BlockSpec((1, tm, GROUP_W), lambda bi, ti: (bi, ti, 0)),
                pl.BlockSpec((1, tm, GROUP_W), lambda bi, ti: (bi, ti, 0)),
                pl.BlockSpec((1, 6, d), lambda bi, ti: (bi, 0, 0)),
                _const_spec((1, d)),
                _const_spec(w_out.shape), _const_spec(w_up.shape),
                _const_spec(conv_w.shape), _const_spec(conv_b.shape),
                _const_spec(w_down.shape)]
    args = [x, y1, y2, mod_l, nw, w_out, w_up, conv_w, conv_b, w_down]
    if final:
        in_specs.append(_const_spec((1, d)))
        args.append(norm_final)
    return pl.pallas_call(
        functools.partial(_ffn_kernel, tm=tm, final=final),
        grid=(bsz, t // tm),
        in_specs=in_specs,
        out_specs=pl.BlockSpec((1, tm, d), lambda bi, ti: (bi, ti, 0)),
        out_shape=jax.ShapeDtypeStruct((bsz, t, d), F32),
        scratch_shapes=[pltpu.VMEM((4, tm + SUBLANES, FF_CHUNK), F32),
                        pltpu.VMEM((2 * (D_FF // FF_CHUNK), SUBLANES, FF_CHUNK), F32),
                        pltpu.VMEM((tm, D_FF), BF16)],
        compiler_params=_cparams("arbitrary", "arbitrary"),
        name="outproj_ffn_final" if final else "outproj_ffn",
    )(*args)


def _head_rows(vec):
    return vec.reshape(-1, 1, LANES)


def kernel(x, c, w_ada, b_ada, norm_mix, norm_ffn, norm_final, w_in_even, b_in_even, conv_qk_w,
           conv_qk_b, mlstm_f_bias, mlstm_norm, hgrn_lb_logits, hgrn_norm, w_in_odd, b_in_odd,
           ret_norm, w_out, ffn_up, ffn_conv_w, ffn_conv_b, ffn_down):
    bsz, t, d = x.shape
    assert d == D_MODEL and t % SB_Q_TILE == 0 and t % REC_CHUNK == 0
    mod = _ada(c, w_ada, b_ada).reshape(DEPTH, bsz, 6, d)
    idx = np.arange(REC_CHUNK)
    tri = jnp.asarray((idx[None, :] <= idx[:, None]).astype(np.float32), BF16)
    gate0 = 4 * GROUP_W
    n_even = hgrn_lb_logits.shape[0]
    lb_rows = hgrn_lb_logits.astype(F32).reshape(n_even, 1, GROUP_W)

    for layer in range(DEPTH):
        mod_l = mod[layer]
        nw_mix = norm_mix[layer][None, :]
        if layer % 2 == 0:
            e = layer // 2
            w, b = w_in_even[e], b_in_even[e]
            w_main = jnp.concatenate([w[:, :gate0], w[:, gate0 + 2 * N_HEADS:]], axis=1).astype(BF16)
            b_main = jnp.concatenate([b[:gate0], b[gate0 + 2 * N_HEADS:]])[None, :]
            pad = LANES - 2 * N_HEADS
            gw = jnp.pad(w[:, gate0:gate0 + 2 * N_HEADS], ((0, 0), (0, pad))).astype(BF16)
            gb = jnp.pad(b[gate0:gate0 + 2 * N_HEADS], (0, pad))[None, :]
            p, gates = _inproj(x, mod_l, nw_mix, w_main, b_main, True, gw, gb,
                               conv_qk_w[e], conv_qk_b[e][None, :])
            fb_row = jnp.pad(mlstm_f_bias[e], (N_HEADS, pad))[None, :]
            y1 = _mlstm(p, gates, fb_row, _head_rows(mlstm_norm[e]), tri)
            y2 = _hgrn(p, lb_rows, _head_rows(hgrn_norm[e]), e)
        else:
            o = layer // 2
            (p,) = _inproj(x, mod_l, nw_mix, w_in_odd[o].astype(BF16), b_in_odd[o][None, :], False)
            y1 = _retention(p, _head_rows(ret_norm[o]))
            y2 = _stick_breaking(p)
        x = _ffn(x, y1, y2, mod_l, norm_ffn[layer][None, :], w_out[layer].astype(BF16),
                 ffn_up[layer].astype(BF16), ffn_conv_w[layer], ffn_conv_b[layer][None, :],
                 ffn_down[layer].astype(BF16),
                 norm_final[None, :] if layer == DEPTH - 1 else None)
    return x
```

```python
import functools

import numpy as np
import jax
import jax.numpy as jnp
from jax import lax
from jax.experimental import pallas as pl
from jax.experimental.pallas import tpu as pltpu

F32 = jnp.float32
BF16 = jnp.bfloat16

D_MODEL = 1024
DEPTH = 4
HEAD_DIM = 128
GROUP_W = D_MODEL // 2
N_HEADS = GROUP_W // HEAD_DIM
SB_HEAD_DIM = 64
CONV_A = 4
FFN_CONV = 3
D_FF = 2816
ROPE_BASE = 10000.0
EPS = 1e-6

LANES = 128
SUBLANES = 8
REC_CHUNK = 128
REC_UNROLL = 2
MLSTM_HEADS_PER_STEP = 4
MLSTM_GATE_UNROLL = 4
HGRN_HEADS_PER_STEP = 4
RET_HEADS_PER_STEP = 4
SB_Q_TILE = 512
SB_K_TILE = 256
SB_STRIP = 16
TOK_TILE = 512
FF_CHUNK = 256
FF_DOWN_GROUP = 4
VMEM_LIMIT = 60 * 1024 * 1024
NEG_BIG = -1e30
LOG2_E = 1.4426950408889634


def _dot(a, b):
    return jnp.dot(a, b, preferred_element_type=F32)


def _dot_tb(a, b):
    return lax.dot_general(a, b, (((1,), (1,)), ((), ())), preferred_element_type=F32)


def _dot_ta(a, b):
    return lax.dot_general(a, b, (((0,), (0,)), ((), ())), preferred_element_type=F32)


def _split3(x):
    hi = x.astype(BF16)
    r1 = x - hi.astype(F32)
    mid = r1.astype(BF16)
    lo = (r1 - mid.astype(F32)).astype(BF16)
    return hi, mid, lo


def _dot_lhs01(w01, x, terms):
    parts = _split3(x)[:terms]
    out = _dot(w01, parts[0])
    for part in parts[1:]:
        out = out + _dot(w01, part)
    return out


def _sigmoid(x):
    return jax.nn.sigmoid(x)


def _silu(x):
    return x * jax.nn.sigmoid(x)


def _log_sigmoid(x):
    return jnp.minimum(x, 0.0) - jnp.log1p(jnp.exp(-jnp.abs(x)))


def _rms_lanes(x):
    return x * lax.rsqrt(jnp.mean(x * x, axis=-1, keepdims=True) + EPS)


def _cparams(*sem):
    return pltpu.CompilerParams(dimension_semantics=sem, vmem_limit_bytes=VMEM_LIMIT)


def _const_spec(shape):
    nd = len(shape)
    return pl.BlockSpec(shape, lambda *_: (0,) * nd, pipeline_mode=pl.Buffered(1))


def _ada_kernel(c_ref, w_ref, b_ref, o_ref):
    c = c_ref[...]
    ada = jnp.dot(_silu(c), w_ref[...], preferred_element_type=F32,
                  precision=lax.Precision.HIGHEST)
    for layer in range(DEPTH):
        o_ref[layer] = ada + b_ref[layer:layer + 1, :]


def _ada(c, w_ada, b_ada):
    bsz, d = c.shape
    n = w_ada.shape[1]
    tn = 1024
    return pl.pallas_call(
        _ada_kernel,
        grid=(n // tn,),
        in_specs=[pl.BlockSpec((bsz, d), lambda j: (0, 0)),
                  pl.BlockSpec((d, tn), lambda j: (0, j)),
                  pl.BlockSpec((DEPTH, tn), lambda j: (0, j))],
        out_specs=pl.BlockSpec((DEPTH, bsz, tn), lambda j: (0, 0, j)),
        out_shape=jax.ShapeDtypeStruct((DEPTH, bsz, n), F32),
        compiler_params=_cparams("arbitrary"),
        name="ada_mod",
    )(c, w_ada, b_ada)


def _modulated_norm(x, nw, shift, scale):
    return _rms_lanes(x) * nw * (1.0 + scale) + shift


def _inproj_kernel(*refs, even, n_cols, tm):
    if even:
        (x_ref, mod_ref, nw_ref, w_ref, b_ref, wg_ref, bg_ref, cw_ref, cb_ref,
         p_ref, g_ref, ext_ref) = refs
    else:
        x_ref, mod_ref, nw_ref, w_ref, b_ref, p_ref = refs
    mod = mod_ref[0]
    h = _modulated_norm(x_ref[0], nw_ref[...], mod[0:1], mod[1:2])
    hb = h.astype(BF16)
    col_chunk = 4 * LANES
    conv_cols = 2 * GROUP_W

    if even:
        @pl.when(pl.program_id(1) == 0)
        def _():
            ext_ref[0:SUBLANES, :] = jnp.zeros((SUBLANES, conv_cols), F32)

    def project(j):
        cols = slice(j * col_chunk, (j + 1) * col_chunk)
        return _dot(hb, w_ref[:, cols]) + b_ref[:, cols]

    n_chunks = n_cols // col_chunk
    ahead = project(0)
    for j in range(n_chunks):
        cols = slice(j * col_chunk, (j + 1) * col_chunk)
        res = ahead
        if j + 1 < n_chunks:
            ahead = project(j + 1)
        if even and (j + 1) * col_chunk <= conv_cols:
            ext_ref[SUBLANES:SUBLANES + tm, cols] = res
            acc = cb_ref[:, cols]
            for i in range(CONV_A):
                off = SUBLANES - (CONV_A - 1) + i
                acc = acc + cw_ref[i:i + 1, cols] * ext_ref[off:off + tm, cols]
            ext_ref[0:SUBLANES, cols] = ext_ref[tm:tm + SUBLANES, cols]
            res = _silu(acc)
        for i in range(col_chunk // LANES):
            p_ref[0, j * (col_chunk // LANES) + i] = res[:, i * LANES:(i + 1) * LANES]

    if even:
        g_ref[0] = _dot(hb, wg_ref[...]) + bg_ref[...]


def _inproj(x, mod_l, nw, w, b, even, gate_w=None, gate_b=None, conv_w=None, conv_b=None):
    bsz, t, d = x.shape
    n_cols = w.shape[1]
    nblk = n_cols // LANES
    tm = min(TOK_TILE, t)
    in_specs = [pl.BlockSpec((1, tm, d), lambda bi, ti: (bi, ti, 0)),
                pl.BlockSpec((1, 6, d), lambda bi, ti: (bi, 0, 0)),
                _const_spec((1, d)),
                _const_spec((d, n_cols)),
                _const_spec((1, n_cols))]
    args = [x, mod_l, nw, w, b]
    out_specs = [pl.BlockSpec((1, nblk, tm, LANES), lambda bi, ti: (bi, 0, ti, 0))]
    out_shape = [jax.ShapeDtypeStruct((bsz, nblk, t, LANES), F32)]
    scratch = []
    if even:
        in_specs += [_const_spec((d, LANES)), _const_spec((1, LANES)),
                     _const_spec((CONV_A, 2 * GROUP_W)), _const_spec((1, 2 * GROUP_W))]
        args += [gate_w, gate_b, conv_w, conv_b]
        out_specs.append(pl.BlockSpec((1, tm, LANES), lambda bi, ti: (bi, ti, 0)))
        out_shape.append(jax.ShapeDtypeStruct((bsz, t, LANES), F32))
        scratch.append(pltpu.VMEM((tm + SUBLANES, 2 * GROUP_W), F32))
    return pl.pallas_call(
        functools.partial(_inproj_kernel, even=even, n_cols=n_cols, tm=tm),
        grid=(bsz, t // tm),
        in_specs=in_specs,
        out_specs=out_specs,
        out_shape=out_shape,
        scratch_shapes=scratch,
        compiler_params=_cparams("arbitrary", "arbitrary"),
        name="inproj_even" if even else "inproj_odd",
    )(*args)


def _mlstm_kernel(q_ref, k_ref, v_ref, o_ref, g_ref, fb_ref, nw_ref, tri_ref, y_ref,
                  arow_ref, cum_ref, gmax_ref, m_ref, *, seq, hb):
    L = REC_CHUNK
    assert L == LANES
    n_chunks = seq // L
    row = lax.broadcasted_iota(jnp.int32, (L, L), 0)
    col = lax.broadcasted_iota(jnp.int32, (L, L), 1)
    causal = col <= row
    ones = jnp.ones((L, HEAD_DIM), BF16)

    def gate_pass(c, _):
        rows = pl.ds(pl.multiple_of(c * L, L), L)
        gates = g_ref[0, rows, :] + fb_ref[...]
        cum_t = _dot_lhs01(tri_ref[...], _log_sigmoid(gates), 3)
        a_t = (gates - pltpu.roll(cum_t, LANES - N_HEADS, 1)).T
        arow_ref[c] = a_t[0:SUBLANES, :]
        for h in range(hb):
            cum_ref[h, rows, :] = jnp.broadcast_to(cum_t[:, N_HEADS + h:N_HEADS + h + 1], (L, LANES))
            a_b = jnp.broadcast_to(a_t[h:h + 1, :], (L, L))
            run_max = jnp.max(jnp.where(causal, a_b, NEG_BIG), axis=1, keepdims=True)
            gmax_ref[h, rows, :] = jnp.broadcast_to(run_max, (L, LANES))
        return 0

    lax.fori_loop(0, n_chunks, gate_pass, 0, unroll=MLSTM_GATE_UNROLL)

    for h in range(hb):
        m = jnp.zeros((1, LANES), F32)
        for c in range(n_chunks):
            m_ref[h * n_chunks + c:h * n_chunks + c + 1, :] = m
            last = slice(c * L + L - 1, c * L + L)
            m = cum_ref[h, last, :] + jnp.maximum(gmax_ref[h, last, :], m)

    def twice(x):
        return jnp.concatenate([x, x], axis=1)

    def body(c, carry):
        rows = pl.ds(pl.multiple_of(c * L, L), L)
        heads = range(hb)
        k = [k_ref[0, h, rows, :] * (HEAD_DIM ** -0.5) for h in heads]
        qb = [q_ref[0, h, rows, :].astype(BF16) for h in heads]
        v_aug = [jnp.concatenate([v_ref[0, h, rows, :].astype(BF16), ones], axis=1) for h in heads]
        k_t = [k[h].T for h in heads]
        s_raw = [_dot_tb(qb[h], k[h].astype(BF16)) for h in heads]
        inter = [_dot(qb[h], carry[h].astype(BF16)) for h in heads]
        a_rows = arow_ref[c]
        m_prev, g, s_dec, kw_t, decay = [], [], [], [], []
        for h in heads:
            m_prev.append(m_ref[pl.ds(h * n_chunks + c, 1), :])
            g.append(jnp.maximum(gmax_ref[h, rows, :], m_prev[h]))
            a_row = a_rows[h:h + 1, :]
            a_b = jnp.broadcast_to(a_row, (L, L))
            s_dec.append((s_raw[h] * jnp.exp(jnp.where(causal, a_b - g[h], NEG_BIG))).astype(BF16))
            g_last = g[h][L - 1:L, :]
            kw_t.append((k_t[h] * jnp.exp(a_row - g_last)).astype(BF16))
            decay.append(jnp.exp(m_prev[h] - g_last))
        intra = [_dot(s_dec[h], v_aug[h]) for h in heads]
        new = tuple(twice(decay[h]) * carry[h] + _dot(kw_t[h], v_aug[h]) for h in heads)
        for h in heads:
            r = intra[h] + twice(jnp.exp(m_prev[h] - g[h])) * inter[h]
            num = r[:, :HEAD_DIM]
            den = r[:, HEAD_DIM:]
            hid = num / jnp.maximum(jnp.abs(den), jnp.exp(-(cum_ref[h, rows, :] + g[h])))
            y = _rms_lanes(hid) * nw_ref[h] * _sigmoid(o_ref[0, h, rows, :])
            y_ref[0, rows, h * HEAD_DIM:(h + 1) * HEAD_DIM] = y.astype(y_ref.dtype)
        return new

    lax.fori_loop(0, n_chunks, body, tuple(jnp.zeros((HEAD_DIM, 2 * HEAD_DIM), F32) for _ in range(hb)))


def _head_spec(t, hb, off):
    assert off % hb == 0
    return pl.BlockSpec((1, hb, t, LANES), lambda bi, hi: (bi, off // hb + hi, 0, 0))


def _mixer_out(bsz, t, hb):
    return (pl.BlockSpec((1, t, hb * LANES), lambda bi, hi: (bi, 0, hi)),
            jax.ShapeDtypeStruct((bsz, t, GROUP_W), BF16))


def _mlstm(p, gates, f_bias_row, norm_rows, tri):
    bsz, _, t, _ = p.shape
    hb = MLSTM_HEADS_PER_STEP
    assert hb == N_HEADS
    out_spec, out_shape = _mixer_out(bsz, t, hb)
    return pl.pallas_call(
        functools.partial(_mlstm_kernel, seq=t, hb=hb),
        grid=(bsz, N_HEADS // hb),
        in_specs=[_head_spec(t, hb, 0), _head_spec(t, hb, N_HEADS), _head_spec(t, hb, 2 * N_HEADS),
                  _head_spec(t, hb, 3 * N_HEADS),
                  pl.BlockSpec((1, t, LANES), lambda bi, hi: (bi, 0, 0)),
                  _const_spec((1, LANES)),
                  pl.BlockSpec((hb, 1, LANES), lambda bi, hi: (hi, 0, 0)),
                  _const_spec((REC_CHUNK, REC_CHUNK))],
        out_specs=out_spec,
        out_shape=out_shape,
        scratch_shapes=[pltpu.VMEM((t // REC_CHUNK, SUBLANES, REC_CHUNK), F32),
                        pltpu.VMEM((hb, t, LANES), F32), pltpu.VMEM((hb, t, LANES), F32),
                        pltpu.VMEM((hb * (t // REC_CHUNK), LANES), F32)],
        compiler_params=_cparams("arbitrary", "arbitrary"),
        name="mlstm",
    )(p, p, p, p, gates, f_bias_row, norm_rows, tri)


def _hgrn_tables():
    L = REC_CHUNK
    t = np.arange(L)
    blocks = [t[None, :] <= t[:, None],
              t[None, :] > t[:, None]]
    masks, signs = [], []
    lev = 0
    while (1 << lev) < L:
        h = 1 << lev
        blk = t // (2 * h)
        upper = (t % (2 * h)) >= h
        mid = blk * 2 * h + h
        eq = upper[:, None] & (t[None, :] >= mid[:, None]) & (t[None, :] <= t[:, None])
        ek = (~upper)[:, None] & (t[None, :] > t[:, None]) & (t[None, :] < mid[:, None])
        if h < SUBLANES:
            blocks.append(eq | ek)
        else:
            signs.append(np.where(upper, 1.0, -1.0))
        masks.append((blk[:, None] == blk[None, :]) & upper[:, None] & (~upper)[None, :])
        lev += 1
    masks.append(t[:, None] == t[None, :])
    w = np.concatenate(blocks, axis=0).astype(np.float32)
    m = np.stack(masks).astype(np.float32)
    s = np.broadcast_to(np.stack(signs)[:, :, None], (len(signs), L, LANES)).astype(np.float32)
    return w, m, s, lev


def _hgrn_kernel(q_ref, f_ref, i_ref, g_ref, lbl_ref, nw_ref, w_ref, m_ref, s_ref, y_ref,
                 *, seq, n_lev, layer_e, hb):
    L = REC_CHUNK
    n_fine = min(n_lev, SUBLANES.bit_length() - 1)

    def lower_bound(h):
        sm = jax.nn.softmax(lbl_ref[:, 0, h * HEAD_DIM:(h + 1) * HEAD_DIM], axis=0)
        lb = jnp.zeros((1, HEAD_DIM), F32)
        for r in range(1, layer_e + 1):
            lb = lb + sm[r:r + 1, :]
        return lb

    lbs = [lower_bound(h) for h in range(hb)]

    def gates(h, rows):
        lb = lbs[h]
        lb_pos = lb > 0.0
        log_lb = jnp.log(jnp.where(lb_pos, lb, 1.0))
        z = f_ref[0, h, rows, :]
        ls = jnp.log1p(-lb) + _log_sigmoid(z)
        lae = jnp.maximum(log_lb, ls) + jnp.log1p(jnp.exp(-jnp.abs(log_lb - ls)))
        return jnp.where(lb_pos, lae, ls), (1.0 - lb) * _sigmoid(-z)

    def body(c, carry):
        rows = pl.ds(pl.multiple_of(c * L, L), L)
        log_f, k, q, v, raw, inter, scores, intra, dec, new_state = ({} for _ in range(10))

        def exponents(h):
            log_f[h], k[h] = gates(h, rows)
            raw[h] = _dot_lhs01(w_ref[...], log_f[h], 2)

        def level_scores(h):
            dec[h] = jnp.exp(raw[h])
            q[h] = _silu(q_ref[0, h, rows, :])
            v[h] = i_ref[0, h, rows, :].astype(BF16)
            inter[h] = _dot_tb((q[h] * dec[h][0:L]).astype(BF16), carry[h].astype(BF16))
            parts = [_dot_tb(q[h].astype(BF16), k[h].astype(BF16))]
            cum = raw[h][0:L]
            for lev in range(n_lev):
                half = 1 << lev
                if half < SUBLANES:
                    fac = dec[h][(2 + lev) * L:(3 + lev) * L]
                else:
                    ref = jnp.concatenate(
                        [jnp.broadcast_to(cum[b * 2 * half + half - 1:b * 2 * half + half, :], (2 * half, HEAD_DIM))
                         for b in range(L // (2 * half))], axis=0)
                    fac = jnp.exp((cum - ref) * s_ref[lev - n_fine])
                parts.append(_dot_tb((q[h] * fac).astype(BF16), (k[h] * fac).astype(BF16)))
            scores[h] = parts

        def mix_values(h):
            acc = scores[h][0] * m_ref[n_lev]
            for lev in range(n_lev):
                acc = acc + scores[h][1 + lev] * m_ref[lev]
            intra[h] = _dot(acc.astype(BF16), v[h])

        def emit(h):
            y = _rms_lanes(inter[h] + intra[h]) * nw_ref[h] * _silu(g_ref[0, h, rows, :])
            y_ref[0, rows, h * HEAD_DIM:(h + 1) * HEAD_DIM] = y.astype(y_ref.dtype)
            new_state[h] = (dec[h][L - 1:L, :] * carry[h]
                            + _dot_ta(v[h], (k[h] * dec[h][L:2 * L]).astype(BF16)))

        stages = (exponents, level_scores, mix_values, emit)
        for step in range(hb + len(stages) - 1):
            for lag, stage in enumerate(stages):
                if 0 <= step - lag < hb:
                    stage(step - lag)
        return tuple(new_state[h] for h in range(hb))

    lax.fori_loop(0, seq // L, body, tuple(jnp.zeros((HEAD_DIM, HEAD_DIM), F32) for _ in range(hb)),
                  unroll=REC_UNROLL)


def _hgrn(p, lb_logits_rows, norm_rows, layer_e):
    bsz, _, t, _ = p.shape
    hb = HGRN_HEADS_PER_STEP
    w_np, m_np, s_np, n_lev = _hgrn_tables()
    w_tab = jnp.asarray(w_np, BF16)
    m_tab = jnp.asarray(m_np, F32)
    s_tab = jnp.asarray(s_np, F32)
    n_even = lb_logits_rows.shape[0]
    out_spec, out_shape = _mixer_out(bsz, t, hb)
    return pl.pallas_call(
        functools.partial(_hgrn_kernel, seq=t, n_lev=n_lev, layer_e=layer_e, hb=hb),
        grid=(bsz, N_HEADS // hb),
        in_specs=[_head_spec(t, hb, 4 * N_HEADS), _head_spec(t, hb, 5 * N_HEADS),
                  _head_spec(t, hb, 6 * N_HEADS), _head_spec(t, hb, 7 * N_HEADS),
                  pl.BlockSpec((n_even, 1, hb * LANES), lambda bi, hi: (0, 0, hi)),
                  pl.BlockSpec((hb, 1, LANES), lambda bi, hi: (hi, 0, 0)),
                  _const_spec(w_tab.shape), _const_spec(m_tab.shape), _const_spec(s_tab.shape)],
        out_specs=out_spec,
        out_shape=out_shape,
        compiler_params=_cparams("arbitrary", "arbitrary"),
        name="hgrn2",
    )(p, p, p, p, lb_logits_rows, norm_rows, w_tab, m_tab, s_tab)


def _retention_kernel(q_ref, k_ref, v_ref, g_ref, cos_ref, sin_ref, lg_ref, nw_ref, y_ref, *, seq, hb):
    L = REC_CHUNK
    row = lax.broadcasted_iota(jnp.int32, (L, L), 0)
    col = lax.broadcasted_iota(jnp.int32, (L, L), 1)
    dist = (row - col).astype(F32)
    pos = lax.broadcasted_iota(jnp.int32, (L, HEAD_DIM), 0).astype(F32)

    def decays(h):
        lg = lg_ref[h]
        return (jnp.where(col <= row, jnp.exp(dist * lg[:, 0:1]), 0.0),
                jnp.exp((pos + 1.0) * lg),
                jnp.exp((L - 1.0 - pos) * lg),
                jnp.exp(L * lg))

    dec = [decays(h) for h in range(hb)]

    def rot(x, cs, sn):
        return x * cs + pltpu.roll(x, HEAD_DIM // 2, 1) * sn

    def body(c, carry):
        rows = pl.ds(pl.multiple_of(c * L, L), L)
        cs = cos_ref[rows, :]
        sn = sin_ref[rows, :]
        heads = range(hb)
        q = [rot(q_ref[0, h, rows, :], cs, sn) for h in heads]
        k = [rot(k_ref[0, h, rows, :], cs, sn) * (HEAD_DIM ** -0.5) for h in heads]
        v = [v_ref[0, h, rows, :].astype(BF16) for h in heads]
        s_raw = [_dot_tb(q[h].astype(BF16), k[h].astype(BF16)) for h in heads]
        inter = [_dot_tb((q[h] * dec[h][1]).astype(BF16), carry[h].astype(BF16)) for h in heads]
        intra = [_dot((s_raw[h] * dec[h][0]).astype(BF16), v[h]) for h in heads]
        for h in heads:
            y = _rms_lanes(inter[h] + intra[h]) * nw_ref[h] * _silu(g_ref[0, h, rows, :])
            y_ref[0, rows, h * HEAD_DIM:(h + 1) * HEAD_DIM] = y.astype(y_ref.dtype)
        return tuple(dec[h][3] * carry[h] + _dot_ta(v[h], (k[h] * dec[h][2]).astype(BF16)) for h in heads)

    lax.fori_loop(0, seq // L, body, tuple(jnp.zeros((HEAD_DIM, HEAD_DIM), F32) for _ in range(hb)),
                  unroll=REC_UNROLL)


def _retention(p, norm_rows):
    bsz, _, t, _ = p.shape
    hb = RET_HEADS_PER_STEP
    inv_freq = ROPE_BASE ** (-np.arange(0, HEAD_DIM, 2, dtype=np.float32) / HEAD_DIM)
    ang = jnp.arange(t, dtype=F32)[:, None] * jnp.asarray(inv_freq, F32)[None, :]
    cos, sin = jnp.cos(ang), jnp.sin(ang)
    cos2 = jnp.concatenate([cos, cos], axis=-1)
    sin2 = jnp.concatenate([-sin, sin], axis=-1)
    log_gamma = np.log1p(-(2.0 ** (-5.0 - np.arange(N_HEADS, dtype=np.float32)))).astype(np.float32)
    lg_rows = jnp.asarray(np.broadcast_to(log_gamma[:, None, None], (N_HEADS, 1, LANES)).copy())
    out_spec, out_shape = _mixer_out(bsz, t, hb)
    per_head = pl.BlockSpec((hb, 1, LANES), lambda bi, hi: (hi, 0, 0))
    return pl.pallas_call(
        functools.partial(_retention_kernel, seq=t, hb=hb),
        grid=(bsz, N_HEADS // hb),
        in_specs=[_head_spec(t, hb, 0), _head_spec(t, hb, N_HEADS), _head_spec(t, hb, 2 * N_HEADS),
                  _head_spec(t, hb, 3 * N_HEADS),
                  _const_spec((t, LANES)), _const_spec((t, LANES)), per_head, per_head],
        out_specs=out_spec,
        out_shape=out_shape,
        compiler_params=_cparams("arbitrary", "arbitrary"),
        name="retention",
    )(p, p, p, p, cos2, sin2, lg_rows, norm_rows)


def _sb_kernel(q_ref, k_ref, v_ref, su_ref, y_ref, z0_ref, z1_ref, lhs0_ref, lhs1_ref, w0_ref, w1_ref,
               sum0_ref, sum1_ref, carry_ref, acc_ref):
    z_bufs, lhs_bufs = (z0_ref, z1_ref), (lhs0_ref, lhs1_ref)
    w_bufs, sum_bufs = (w0_ref, w1_ref), (sum0_ref, sum1_ref)
    tq, tk, rs = SB_Q_TILE, SB_K_TILE, SB_STRIP
    n_strip = tq // rs
    ratio = tq // tk
    assert ratio == 2
    qi = pl.program_id(2)
    n_full = qi * ratio
    q2 = q_ref[0, 0] * (SB_HEAD_DIM ** -0.5 * LOG2_E)
    first = lax.broadcasted_iota(jnp.int32, (tq, LANES), 1) < SB_HEAD_DIM
    q_cat = jnp.concatenate([jnp.where(first, q2, 0.0), jnp.where(first, 0.0, q2)], axis=0).astype(BF16)
    row = lax.broadcasted_iota(jnp.int32, (rs, tk), 0)
    col = lax.broadcasted_iota(jnp.int32, (rs, tk), 1)

    def tile_of(idx):
        if isinstance(idx, int) and idx < ratio:
            return qi * ratio + (ratio - 1 - idx), (ratio - 1 - idx) * tk
        return n_full - 1 - (idx - ratio), None

    def rows_of(idx):
        return pl.ds(pl.multiple_of(tile_of(idx)[0] * tk, tk), tk)

    def visible(s, key_off):
        return col + key_off < row + (s % n_strip) * rs

    def stage_a(idx, slot):
        z_bufs[slot][...] = _dot_tb(q_cat, k_ref[0, 0, rows_of(idx), :].astype(BF16))

    def stage_b(idx, slot):
        key_off = tile_of(idx)[1]
        z_buf, lhs_buf, w_buf = z_bufs[slot], lhs_bufs[slot], w_bufs[slot]
        for s in range(2 * n_strip):
            strip = slice(s * rs, (s + 1) * rs)
            zs = z_buf[strip, :]
            sp = jnp.maximum(zs, 0.0) + jnp.log2(1.0 + jnp.exp2(-jnp.abs(zs)))
            if key_off is not None:
                sp = jnp.where(visible(s, key_off), sp, 0.0)
            hi = sp.astype(BF16)
            lhs_buf[strip, 0:tk] = hi
            lhs_buf[strip, tk:2 * tk] = (sp - hi.astype(F32)).astype(BF16)
        incl = _dot(lhs_buf[...], su_ref[...])
        for s in range(2 * n_strip):
            strip = slice(s * rs, (s + 1) * rs)
            arg = z_buf[strip, :] - incl[strip]
            if key_off is not None:
                arg = jnp.where(visible(s, key_off), arg, NEG_BIG)
            w_buf[strip, :] = jnp.exp2(arg).astype(BF16)
        sum_bufs[slot][...] = jnp.broadcast_to(incl[:, 0:1], (2 * tq, LANES))

    def stage_c(idx, slot):
        v2 = v_ref[0, 0, rows_of(idx), :].astype(BF16)
        carry = carry_ref[...]
        acc_ref[...] += jnp.exp2(-carry) * _dot(w_bufs[slot][...], v2)
        carry_ref[...] = carry + sum_bufs[slot][...]

    def finish():
        y_ref[0] = jnp.where(first, acc_ref[0:tq, :], acc_ref[tq:2 * tq, :]).astype(y_ref.dtype)

    carry_ref[...] = jnp.zeros((2 * tq, LANES), F32)
    acc_ref[...] = jnp.zeros((2 * tq, LANES), F32)
    stage_a(0, 0)
    stage_a(1, 1)
    stage_b(0, 0)

    @pl.when(qi == 0)
    def _():
        stage_b(1, 1)
        stage_c(0, 0)
        stage_c(1, 1)
        finish()

    @pl.when(qi > 0)
    def _():
        stage_a(2, 0)
        stage_b(1, 1)
        stage_c(0, 0)
        stage_a(3, 1)
        stage_b(2, 0)
        stage_c(1, 1)

        def two_steps(i, _):
            n = 2 + 2 * i
            stage_a(n + 2, 0)
            stage_b(n + 1, 1)
            stage_c(n, 0)
            stage_a(n + 3, 1)
            stage_b(n + 2, 0)
            stage_c(n + 1, 1)
            return 0

        lax.fori_loop(0, qi - 1, two_steps, 0)
        last = ratio + n_full - 1
        stage_b(last, 1)
        stage_c(last - 1, 0)
        stage_c(last, 1)
        finish()


def _stick_breaking(p):
    bsz, _, t, _ = p.shape
    tq, tk = SB_Q_TILE, SB_K_TILE
    n_pair = GROUP_W // LANES
    idx = np.arange(tk)
    from_s = (idx[:, None] >= idx[None, :]).astype(np.float32)
    su = jnp.asarray(np.concatenate([from_s, from_s], axis=0), BF16)
    return pl.pallas_call(
        _sb_kernel,
        grid=(bsz, n_pair, t // tq),
        in_specs=[pl.BlockSpec((1, 1, tq, LANES), lambda bi, hi, qi: (bi, 4 * N_HEADS + hi, qi, 0)),
                  pl.BlockSpec((1, 1, t, LANES), lambda bi, hi, qi: (bi, 4 * N_HEADS + n_pair + hi, 0, 0)),
                  pl.BlockSpec((1, 1, t, LANES), lambda bi, hi, qi: (bi, 4 * N_HEADS + 2 * n_pair + hi, 0, 0)),
                  _const_spec((2 * tk, tk))],
        out_specs=pl.BlockSpec((1, tq, LANES), lambda bi, hi, qi: (bi, qi, hi)),
        out_shape=jax.ShapeDtypeStruct((bsz, t, GROUP_W), BF16),
        scratch_shapes=(2 * [pltpu.VMEM((2 * tq, tk), F32)] + 2 * [pltpu.VMEM((2 * tq, 2 * tk), BF16)]
                        + 2 * [pltpu.VMEM((2 * tq, tk), BF16)] + 4 * [pltpu.VMEM((2 * tq, LANES), F32)]),
        compiler_params=_cparams("arbitrary", "arbitrary", "arbitrary"),
        name="stick_breaking",
    )(p, p, p, su)


def _ffn_kernel(*refs, tm, final):
    if final:
        (x_ref, y1_ref, y2_ref, mod_ref, nw_ref, wo_ref, wu_ref, cw_ref, cb_ref, wd_ref, nf_ref,
         o_ref, ext_ref, carry_ref, act_ref, xs_ref) = refs
    else:
        (x_ref, y1_ref, y2_ref, mod_ref, nw_ref, wo_ref, wu_ref, cw_ref, cb_ref, wd_ref,
         o_ref, ext_ref, carry_ref, act_ref, xs_ref) = refs
    fc = FF_CHUNK
    n_chunks = D_FF // fc

    @pl.when(pl.program_id(1) == 0)
    def _():
        carry_ref[...] = jnp.zeros(carry_ref.shape, F32)

    depth = tm // SUBLANES
    halo = FFN_CONV - 1
    mod = mod_ref[0]
    mix = _dot(jnp.concatenate([y1_ref[0], y2_ref[0]], axis=1), wo_ref[...])
    n_lane_tiles = D_MODEL // LANES

    def restage(val, start_of, stride):
        for ct in range(n_lane_tiles):
            xs_ref[ct] = val[:, ct * LANES:(ct + 1) * LANES]
        return jnp.concatenate(
            [jnp.concatenate([xs_ref[ct, pl.ds(start_of(g), SUBLANES, stride=stride), :]
                              for ct in range(n_lane_tiles)], axis=1) for g in range(depth)], axis=0)

    x1 = restage(x_ref[0] + mod[2:3] * mix, lambda g: g, depth)
    hb = _modulated_norm(x1, nw_ref[...], mod[3:4], mod[4:5]).astype(BF16)
    acc = jnp.zeros((tm, D_MODEL), F32)
    sublane0 = lax.broadcasted_iota(jnp.int32, (SUBLANES, 2 * fc), 0) == 0

    def cols_of(j):
        return slice(2 * j * fc, 2 * (j + 1) * fc)

    def up_project(j):
        ext = ext_ref.at[j % 2]
        u = _dot(hb, wu_ref[:, cols_of(j)])
        ext[halo * SUBLANES:halo * SUBLANES + tm, :] = u
        tail = u[tm - halo * SUBLANES:tm]
        prev = carry_ref[j]
        for r in range(halo):
            grp = slice(r * SUBLANES, (r + 1) * SUBLANES)
            ext[grp, :] = jnp.where(sublane0, pltpu.roll(prev[grp], 1, 0), pltpu.roll(tail[grp], 1, 0))
        carry_ref[j] = tail

    def down_project(acc, ks):
        return acc + _dot(act_ref[:, ks], wd_ref[ks, :])

    up_project(0)
    pending = None
    for j in range(n_chunks):
        if j + 1 < n_chunks:
            up_project(j + 1)
        if pending is not None:
            acc = down_project(acc, pending)
            pending = None
        cols = cols_of(j)
        ext = ext_ref.at[j % 2]
        u = cb_ref[:, cols]
        for i in range(FFN_CONV):
            u = u + cw_ref[i:i + 1, cols] * ext[i * SUBLANES:i * SUBLANES + tm, :]
        act_ref[:, j * fc:(j + 1) * fc] = (u[:, :fc] * _silu(u[:, fc:])).astype(BF16)
        if (j + 1) % FF_DOWN_GROUP == 0 or j + 1 == n_chunks:
            pending = slice((j // FF_DOWN_GROUP) * FF_DOWN_GROUP * fc, (j + 1) * fc)
    x2 = x1 + mod[5:6] * down_project(acc, pending)
    if final:
        x2 = _rms_lanes(x2) * nf_ref[...]
    o_ref[0] = restage(x2, lambda g: ((g * SUBLANES) % depth) * SUBLANES + (g * SUBLANES) // depth, SUBLANES)


def _ffn(x, y1, y2, mod_l, nw, w_out, w_up, conv_w, conv_b, w_down, norm_final=None):
    bsz, t, d = x.shape
    tm = min(TOK_TILE, t)
    final = norm_final is not None
    in_specs = [pl.BlockSpec((1, tm, d), lambda bi, ti: (bi, ti, 0)),
                pl.BlockSpec((1, tm, GROUP_W), lambda bi, ti: (bi, ti, 0)),
                pl.BlockSpec((1, tm, GROUP_W), lambda bi, ti: (bi, ti, 0)),
                pl.BlockSpec((1, 6, d), lambda bi, ti: (bi, 0, 0)),
                _const_spec((1, d)),
                _const_spec(w_out.shape), _const_spec(w_up.shape),
                _const_spec(conv_w.shape), _const_spec(conv_b.shape),
                _const_spec(w_down.shape)]
    args = [x, y1, y2, mod_l, nw, w_out, w_up, conv_w, conv_b, w_down]
    if final:
        in_specs.append(_const_spec((1, d)))
        args.append(norm_final)
    return pl.pallas_call(
        functools.partial(_ffn_kernel, tm=tm, final=final),
        grid=(bsz, t // tm),
        in_specs=in_specs,
        out_specs=pl.BlockSpec((1, tm, d), lambda bi, ti: (bi, ti, 0)),
        out_shape=jax.ShapeDtypeStruct((bsz, t, d), F32),
        scratch_shapes=[pltpu.VMEM((2, tm + (FFN_CONV - 1) * SUBLANES, 2 * FF_CHUNK), F32),
                        pltpu.VMEM((D_FF // FF_CHUNK, (FFN_CONV - 1) * SUBLANES, 2 * FF_CHUNK), F32),
                        pltpu.VMEM((tm, D_FF), BF16),
                        pltpu.VMEM((D_MODEL // LANES, tm, LANES), F32)],
        compiler_params=_cparams("arbitrary", "arbitrary"),
        name="outproj_ffn_final" if final else "outproj_ffn",
    )(*args)


def _head_rows(vec):
    return vec.reshape(-1, 1, LANES)


def _chunk_pairs(a):
    lead = a.shape[:-1]
    a = a.reshape(lead + (2, D_FF // FF_CHUNK, FF_CHUNK))
    return jnp.swapaxes(a, -3, -2).reshape(lead + (2 * D_FF,))


def kernel(x, c, w_ada, b_ada, norm_mix, norm_ffn, norm_final, w_in_even, b_in_even, conv_qk_w,
           conv_qk_b, mlstm_f_bias, mlstm_norm, hgrn_lb_logits, hgrn_norm, w_in_odd, b_in_odd,
           ret_norm, w_out, ffn_up, ffn_conv_w, ffn_conv_b, ffn_down):
    bsz, t, d = x.shape
    assert d == D_MODEL and t % SB_Q_TILE == 0 and t % REC_CHUNK == 0
    mod = _ada(c, w_ada, b_ada).reshape(DEPTH, bsz, 6, d)
    idx = np.arange(REC_CHUNK)
    tri = jnp.asarray((idx[None, :] <= idx[:, None]).astype(np.float32), BF16)
    gate0 = 4 * GROUP_W
    n_even = hgrn_lb_logits.shape[0]
    lb_rows = hgrn_lb_logits.astype(F32).reshape(n_even, 1, GROUP_W)

    for layer in range(DEPTH):
        mod_l = mod[layer]
        nw_mix = norm_mix[layer][None, :]
        if layer % 2 == 0:
            e = layer // 2
            w, b = w_in_even[e], b_in_even[e]
            w_main = jnp.concatenate([w[:, :gate0], w[:, gate0 + 2 * N_HEADS:]], axis=1).astype(BF16)
            b_main = jnp.concatenate([b[:gate0], b[gate0 + 2 * N_HEADS:]])[None, :]
            pad = LANES - 2 * N_HEADS
            gw = jnp.pad(w[:, gate0:gate0 + 2 * N_HEADS], ((0, 0), (0, pad))).astype(BF16)
            gb = jnp.pad(b[gate0:gate0 + 2 * N_HEADS], (0, pad))[None, :]
            p, gates = _inproj(x, mod_l, nw_mix, w_main, b_main, True, gw, gb,
                               conv_qk_w[e], conv_qk_b[e][None, :])
            fb_row = jnp.pad(mlstm_f_bias[e], (N_HEADS, pad))[None, :]
            y1 = _mlstm(p, gates, fb_row, _head_rows(mlstm_norm[e]), tri)
            y2 = _hgrn(p, lb_rows, _head_rows(hgrn_norm[e]), e)
        else:
            o = layer // 2
            (p,) = _inproj(x, mod_l, nw_mix, w_in_odd[o].astype(BF16), b_in_odd[o][None, :], False)
            y1 = _retention(p, _head_rows(ret_norm[o]))
            y2 = _stick_breaking(p)
        x = _ffn(x, y1, y2, mod_l, norm_ffn[layer][None, :], w_out[layer].astype(BF16),
                 _chunk_pairs(ffn_up[layer]).astype(BF16), _chunk_pairs(ffn_conv_w[layer]),
                 _chunk_pairs(ffn_conv_b[layer][None, :]), ffn_down[layer].astype(BF16),
                 norm_final[None, :] if layer == DEPTH - 1 else None)
    return x
```

```python
import functools

import numpy as np
import jax
import jax.numpy as jnp
from jax import lax
from jax.experimental import pallas as pl
from jax.experimental.pallas import tpu as pltpu

F32 = jnp.float32
BF16 = jnp.bfloat16

D_MODEL = 1024
DEPTH = 4
HEAD_DIM = 128
GROUP_W = D_MODEL // 2
N_HEADS = GROUP_W // HEAD_DIM
SB_HEAD_DIM = 64
CONV_A = 4
FFN_CONV = 3
D_FF = 2816
ROPE_BASE = 10000.0
EPS = 1e-6

LANES = 128
SUBLANES = 8
REC_CHUNK = 128
REC_UNROLL = 2
MLSTM_HEADS_PER_STEP = 4
MLSTM_GATE_UNROLL = 4
HGRN_HEADS_PER_STEP = 4
RET_HEADS_PER_STEP = 4
SB_Q_TILE = 512
SB_K_TILE = 256
SB_STRIP = 16
TOK_TILE = 512
FF_CHUNK = 256
FF_DOWN_GROUP = 4
VMEM_LIMIT = 60 * 1024 * 1024
NEG_BIG = -1e30
LOG2_E = 1.4426950408889634


def _dot(a, b):
    return jnp.dot(a, b, preferred_element_type=F32)


def _dot_tb(a, b):
    return lax.dot_general(a, b, (((1,), (1,)), ((), ())), preferred_element_type=F32)


def _dot_ta(a, b):
    return lax.dot_general(a, b, (((0,), (0,)), ((), ())), preferred_element_type=F32)


def _split3(x):
    hi = x.astype(BF16)
    r1 = x - hi.astype(F32)
    mid = r1.astype(BF16)
    lo = (r1 - mid.astype(F32)).astype(BF16)
    return hi, mid, lo


def _dot_lhs01(w01, x, terms):
    parts = _split3(x)[:terms]
    out = _dot(w01, parts[0])
    for part in parts[1:]:
        out = out + _dot(w01, part)
    return out


def _sigmoid(x):
    return jax.nn.sigmoid(x)


def _silu(x):
    return x * jax.nn.sigmoid(x)


def _log_sigmoid(x):
    return jnp.minimum(x, 0.0) - jnp.log1p(jnp.exp(-jnp.abs(x)))


def _rms_lanes(x):
    return x * lax.rsqrt(jnp.mean(x * x, axis=-1, keepdims=True) + EPS)


def _cparams(*sem):
    return pltpu.CompilerParams(dimension_semantics=sem, vmem_limit_bytes=VMEM_LIMIT)


def _const_spec(shape):
    nd = len(shape)
    return pl.BlockSpec(shape, lambda *_: (0,) * nd, pipeline_mode=pl.Buffered(1))


def _ada_kernel(c_ref, w_ref, b_ref, o_ref):
    c = c_ref[...]
    ada = jnp.dot(_silu(c), w_ref[...], preferred_element_type=F32,
                  precision=lax.Precision.HIGHEST)
    for layer in range(DEPTH):
        o_ref[layer] = ada + b_ref[layer:layer + 1, :]


def _ada(c, w_ada, b_ada):
    bsz, d = c.shape
    n = w_ada.shape[1]
    tn = 1024
    return pl.pallas_call(
        _ada_kernel,
        grid=(n // tn,),
        in_specs=[pl.BlockSpec((bsz, d), lambda j: (0, 0)),
                  pl.BlockSpec((d, tn), lambda j: (0, j)),
                  pl.BlockSpec((DEPTH, tn), lambda j: (0, j))],
        out_specs=pl.BlockSpec((DEPTH, bsz, tn), lambda j: (0, 0, j)),
        out_shape=jax.ShapeDtypeStruct((DEPTH, bsz, n), F32),
        compiler_params=_cparams("arbitrary"),
        name="ada_mod",
    )(c, w_ada, b_ada)


def _modulated_norm(x, nw, shift, scale):
    return _rms_lanes(x) * nw * (1.0 + scale) + shift


def _inproj_kernel(*refs, even, n_cols, tm):
    if even:
        (x_ref, mod_ref, nw_ref, w_ref, b_ref, wg_ref, bg_ref, cw_ref, cb_ref,
         p_ref, g_ref, ext_ref) = refs
    else:
        x_ref, mod_ref, nw_ref, w_ref, b_ref, p_ref = refs
    mod = mod_ref[0]
    h = _modulated_norm(x_ref[0], nw_ref[...], mod[0:1], mod[1:2])
    hb = h.astype(BF16)
    col_chunk = 4 * LANES
    conv_cols = 2 * GROUP_W

    if even:
        @pl.when(pl.program_id(1) == 0)
        def _():
            ext_ref[0:SUBLANES, :] = jnp.zeros((SUBLANES, conv_cols), F32)

    def project(j):
        cols = slice(j * col_chunk, (j + 1) * col_chunk)
        return _dot(hb, w_ref[:, cols]) + b_ref[:, cols]

    n_chunks = n_cols // col_chunk
    ahead = project(0)
    for j in range(n_chunks):
        cols = slice(j * col_chunk, (j + 1) * col_chunk)
        res = ahead
        if j + 1 < n_chunks:
            ahead = project(j + 1)
        if even and (j + 1) * col_chunk <= conv_cols:
            ext_ref[SUBLANES:SUBLANES + tm, cols] = res
            acc = cb_ref[:, cols]
            for i in range(CONV_A):
                off = SUBLANES - (CONV_A - 1) + i
                acc = acc + cw_ref[i:i + 1, cols] * ext_ref[off:off + tm, cols]
            ext_ref[0:SUBLANES, cols] = ext_ref[tm:tm + SUBLANES, cols]
            res = _silu(acc)
        for i in range(col_chunk // LANES):
            p_ref[0, j * (col_chunk // LANES) + i] = res[:, i * LANES:(i + 1) * LANES]

    if even:
        g_ref[0] = _dot(hb, wg_ref[...]) + bg_ref[...]


def _inproj(x, mod_l, nw, w, b, even, gate_w=None, gate_b=None, conv_w=None, conv_b=None):
    bsz, t, d = x.shape
    n_cols = w.shape[1]
    nblk = n_cols // LANES
    tm = min(TOK_TILE, t)
    in_specs = [pl.BlockSpec((1, tm, d), lambda bi, ti: (bi, ti, 0)),
                pl.BlockSpec((1, 6, d), lambda bi, ti: (bi, 0, 0)),
                _const_spec((1, d)),
                _const_spec((d, n_cols)),
                _const_spec((1, n_cols))]
    args = [x, mod_l, nw, w, b]
    out_specs = [pl.BlockSpec((1, nblk, tm, LANES), lambda bi, ti: (bi, 0, ti, 0))]
    out_shape = [jax.ShapeDtypeStruct((bsz, nblk, t, LANES), F32)]
    scratch = []
    if even:
        in_specs += [_const_spec((d, LANES)), _const_spec((1, LANES)),
                     _const_spec((CONV_A, 2 * GROUP_W)), _const_spec((1, 2 * GROUP_W))]
        args += [gate_w, gate_b, conv_w, conv_b]
        out_specs.append(pl.BlockSpec((1, tm, LANES), lambda bi, ti: (bi, ti, 0)))
        out_shape.append(jax.ShapeDtypeStruct((bsz, t, LANES), F32))
        scratch.append(pltpu.VMEM((tm + SUBLANES, 2 * GROUP_W), F32))
    return pl.pallas_call(
        functools.partial(_inproj_kernel, even=even, n_cols=n_cols, tm=tm),
        grid=(bsz, t // tm),
        in_specs=in_specs,
        out_specs=out_specs,
        out_shape=out_shape,
        scratch_shapes=scratch,
        compiler_params=_cparams("arbitrary", "arbitrary"),
        name="inproj_even" if even else "inproj_odd",
    )(*args)


def _mlstm_kernel(q_ref, k_ref, v_ref, o_ref, g_ref, fb_ref, nw_ref, tri_ref, y_ref,
                  arow_ref, cum_ref, gmax_ref, m_ref, *, seq, hb):
    L = REC_CHUNK
    assert L == LANES
    n_chunks = seq // L
    row = lax.broadcasted_iota(jnp.int32, (L, L), 0)
    col = lax.broadcasted_iota(jnp.int32, (L, L), 1)
    causal = col <= row
    ones = jnp.ones((L, HEAD_DIM), BF16)

    def gate_pass(c, _):
        rows = pl.ds(pl.multiple_of(c * L, L), L)
        gates = g_ref[0, rows, :] + fb_ref[...]
        cum_t = _dot_lhs01(tri_ref[...], _log_sigmoid(gates), 3)
        a_t = (gates - pltpu.roll(cum_t, LANES - N_HEADS, 1)).T
        arow_ref[c] = a_t[0:SUBLANES, :]
        for h in range(hb):
            cum_ref[h, rows, :] = jnp.broadcast_to(cum_t[:, N_HEADS + h:N_HEADS + h + 1], (L, LANES))
            a_b = jnp.broadcast_to(a_t[h:h + 1, :], (L, L))
            run_max = jnp.max(jnp.where(causal, a_b, NEG_BIG), axis=1, keepdims=True)
            gmax_ref[h, rows, :] = jnp.broadcast_to(run_max, (L, LANES))
        return 0

    lax.fori_loop(0, n_chunks, gate_pass, 0, unroll=MLSTM_GATE_UNROLL)

    for h in range(hb):
        m = jnp.zeros((1, LANES), F32)
        for c in range(n_chunks):
            m_ref[h * n_chunks + c:h * n_chunks + c + 1, :] = m
            last = slice(c * L + L - 1, c * L + L)
            m = cum_ref[h, last, :] + jnp.maximum(gmax_ref[h, last, :], m)

    def twice(x):
        return jnp.concatenate([x, x], axis=1)

    def body(c, carry):
        rows = pl.ds(pl.multiple_of(c * L, L), L)
        heads = range(hb)
        k = [k_ref[0, h, rows, :] * (HEAD_DIM ** -0.5) for h in heads]
        qb = [q_ref[0, h, rows, :].astype(BF16) for h in heads]
        v_aug = [jnp.concatenate([v_ref[0, h, rows, :].astype(BF16), ones], axis=1) for h in heads]
        k_t = [k[h].T for h in heads]
        s_raw = [_dot_tb(qb[h], k[h].astype(BF16)) for h in heads]
        inter = [_dot(qb[h], carry[h].astype(BF16)) for h in heads]
        a_rows = arow_ref[c]
        m_prev, g, s_dec, kw_t, decay = [], [], [], [], []
        for h in heads:
            m_prev.append(m_ref[pl.ds(h * n_chunks + c, 1), :])
            g.append(jnp.maximum(gmax_ref[h, rows, :], m_prev[h]))
            a_row = a_rows[h:h + 1, :]
            a_b = jnp.broadcast_to(a_row, (L, L))
            s_dec.append((s_raw[h] * jnp.exp(jnp.where(causal, a_b - g[h], NEG_BIG))).astype(BF16))
            g_last = g[h][L - 1:L, :]
            kw_t.append((k_t[h] * jnp.exp(a_row - g_last)).astype(BF16))
            decay.append(jnp.exp(m_prev[h] - g_last))
        intra = [_dot(s_dec[h], v_aug[h]) for h in heads]
        new = tuple(twice(decay[h]) * carry[h] + _dot(kw_t[h], v_aug[h]) for h in heads)
        for h in heads:
            r = intra[h] + twice(jnp.exp(m_prev[h] - g[h])) * inter[h]
            num = r[:, :HEAD_DIM]
            den = r[:, HEAD_DIM:]
            hid = num / jnp.maximum(jnp.abs(den), jnp.exp(-(cum_ref[h, rows, :] + g[h])))
            y = _rms_lanes(hid) * nw_ref[h] * _sigmoid(o_ref[0, h, rows, :])
            y_ref[0, rows, h * HEAD_DIM:(h + 1) * HEAD_DIM] = y.astype(y_ref.dtype)
        return new

    lax.fori_loop(0, n_chunks, body, tuple(jnp.zeros((HEAD_DIM, 2 * HEAD_DIM), F32) for _ in range(hb)))


def _head_spec(t, hb, off):
    assert off % hb == 0
    return pl.BlockSpec((1, hb, t, LANES), lambda bi, hi: (bi, off // hb + hi, 0, 0))


def _mixer_out(bsz, t, hb):
    return (pl.BlockSpec((1, t, hb * LANES), lambda bi, hi: (bi, 0, hi)),
            jax.ShapeDtypeStruct((bsz, t, GROUP_W), BF16))


def _mlstm(p, gates, f_bias_row, norm_rows, tri):
    bsz, _, t, _ = p.shape
    hb = MLSTM_HEADS_PER_STEP
    assert hb == N_HEADS
    out_spec, out_shape = _mixer_out(bsz, t, hb)
    return pl.pallas_call(
        functools.partial(_mlstm_kernel, seq=t, hb=hb),
        grid=(bsz, N_HEADS // hb),
        in_specs=[_head_spec(t, hb, 0), _head_spec(t, hb, N_HEADS), _head_spec(t, hb, 2 * N_HEADS),
                  _head_spec(t, hb, 3 * N_HEADS),
                  pl.BlockSpec((1, t, LANES), lambda bi, hi: (bi, 0, 0)),
                  _const_spec((1, LANES)),
                  pl.BlockSpec((hb, 1, LANES), lambda bi, hi: (hi, 0, 0)),
                  _const_spec((REC_CHUNK, REC_CHUNK))],
        out_specs=out_spec,
        out_shape=out_shape,
        scratch_shapes=[pltpu.VMEM((t // REC_CHUNK, SUBLANES, REC_CHUNK), F32),
                        pltpu.VMEM((hb, t, LANES), F32), pltpu.VMEM((hb, t, LANES), F32),
                        pltpu.VMEM((hb * (t // REC_CHUNK), LANES), F32)],
        compiler_params=_cparams("arbitrary", "arbitrary"),
        name="mlstm",
    )(p, p, p, p, gates, f_bias_row, norm_rows, tri)


def _hgrn_tables():
    L = REC_CHUNK
    t = np.arange(L)
    blocks = [t[None, :] <= t[:, None],
              t[None, :] > t[:, None]]
    masks, signs = [], []
    lev = 0
    while (1 << lev) < L:
        h = 1 << lev
        blk = t // (2 * h)
        upper = (t % (2 * h)) >= h
        mid = blk * 2 * h + h
        eq = upper[:, None] & (t[None, :] >= mid[:, None]) & (t[None, :] <= t[:, None])
        ek = (~upper)[:, None] & (t[None, :] > t[:, None]) & (t[None, :] < mid[:, None])
        if h < SUBLANES:
            blocks.append(eq | ek)
        else:
            signs.append(np.where(upper, 1.0, -1.0))
        masks.append((blk[:, None] == blk[None, :]) & upper[:, None] & (~upper)[None, :])
        lev += 1
    masks.append(t[:, None] == t[None, :])
    w = np.concatenate(blocks, axis=0).astype(np.float32)
    m = np.stack(masks).astype(np.float32)
    s = np.broadcast_to(np.stack(signs)[:, :, None], (len(signs), L, LANES)).astype(np.float32)
    return w, m, s, lev


def _hgrn_kernel(q_ref, f_ref, i_ref, g_ref, lbl_ref, nw_ref, w_ref, m_ref, s_ref, y_ref,
                 *, seq, n_lev, layer_e, hb):
    L = REC_CHUNK
    n_fine = min(n_lev, SUBLANES.bit_length() - 1)

    def lower_bound(h):
        sm = jax.nn.softmax(lbl_ref[:, 0, h * HEAD_DIM:(h + 1) * HEAD_DIM], axis=0)
        lb = jnp.zeros((1, HEAD_DIM), F32)
        for r in range(1, layer_e + 1):
            lb = lb + sm[r:r + 1, :]
        return lb

    lbs = [lower_bound(h) for h in range(hb)]

    def gates(h, rows):
        lb = lbs[h]
        lb_pos = lb > 0.0
        log_lb = jnp.log(jnp.where(lb_pos, lb, 1.0))
        z = f_ref[0, h, rows, :]
        ls = jnp.log1p(-lb) + _log_sigmoid(z)
        lae = jnp.maximum(log_lb, ls) + jnp.log1p(jnp.exp(-jnp.abs(log_lb - ls)))
        return jnp.where(lb_pos, lae, ls), (1.0 - lb) * _sigmoid(-z)

    def body(c, carry):
        rows = pl.ds(pl.multiple_of(c * L, L), L)
        log_f, k, q, v, raw, inter, scores, intra, dec, new_state = ({} for _ in range(10))

        def exponents(h):
            log_f[h], k[h] = gates(h, rows)
            raw[h] = _dot_lhs01(w_ref[...], log_f[h], 2)

        def level_scores(h):
            dec[h] = jnp.exp(raw[h])
            q[h] = _silu(q_ref[0, h, rows, :])
            v[h] = i_ref[0, h, rows, :].astype(BF16)
            inter[h] = _dot_tb((q[h] * dec[h][0:L]).astype(BF16), carry[h].astype(BF16))
            parts = [_dot_tb(q[h].astype(BF16), k[h].astype(BF16))]
            cum = raw[h][0:L]
            for lev in range(n_lev):
                half = 1 << lev
                if half < SUBLANES:
                    fac = dec[h][(2 + lev) * L:(3 + lev) * L]
                else:
                    ref = jnp.concatenate(
                        [jnp.broadcast_to(cum[b * 2 * half + half - 1:b * 2 * half + half, :], (2 * half, HEAD_DIM))
                         for b in range(L // (2 * half))], axis=0)
                    fac = jnp.exp((cum - ref) * s_ref[lev - n_fine])
                parts.append(_dot_tb((q[h] * fac).astype(BF16), (k[h] * fac).astype(BF16)))
            scores[h] = parts

        def mix_values(h):
            acc = scores[h][0] * m_ref[n_lev]
            for lev in range(n_lev):
                acc = acc + scores[h][1 + lev] * m_ref[lev]
            intra[h] = _dot(acc.astype(BF16), v[h])

        def emit(h):
            y = _rms_lanes(inter[h] + intra[h]) * nw_ref[h] * _silu(g_ref[0, h, rows, :])
            y_ref[0, rows, h * HEAD_DIM:(h + 1) * HEAD_DIM] = y.astype(y_ref.dtype)
            new_state[h] = (dec[h][L - 1:L, :] * carry[h]
                            + _dot_ta(v[h], (k[h] * dec[h][L:2 * L]).astype(BF16)))

        stages = (exponents, level_scores, mix_values, emit)
        for step in range(hb + len(stages) - 1):
            for lag, stage in enumerate(stages):
                if 0 <= step - lag < hb:
                    stage(step - lag)
        return tuple(new_state[h] for h in range(hb))

    lax.fori_loop(0, seq // L, body, tuple(jnp.zeros((HEAD_DIM, HEAD_DIM), F32) for _ in range(hb)),
                  unroll=REC_UNROLL)


def _hgrn(p, lb_logits_rows, norm_rows, layer_e):
    bsz, _, t, _ = p.shape
    hb = HGRN_HEADS_PER_STEP
    w_np, m_np, s_np, n_lev = _hgrn_tables()
    w_tab = jnp.asarray(w_np, BF16)
    m_tab = jnp.asarray(m_np, F32)
    s_tab = jnp.asarray(s_np, F32)
    n_even = lb_logits_rows.shape[0]
    out_spec, out_shape = _mixer_out(bsz, t, hb)
    return pl.pallas_call(
        functools.partial(_hgrn_kernel, seq=t, n_lev=n_lev, layer_e=layer_e, hb=hb),
        grid=(bsz, N_HEADS // hb),
        in_specs=[_head_spec(t, hb, 4 * N_HEADS), _head_spec(t, hb, 5 * N_HEADS),
                  _head_spec(t, hb, 6 * N_HEADS), _head_spec(t, hb, 7 * N_HEADS),
                  pl.BlockSpec((n_even, 1, hb * LANES), lambda bi, hi: (0, 0, hi)),
                  pl.BlockSpec((hb, 1, LANES), lambda bi, hi: (hi, 0, 0)),
                  _const_spec(w_tab.shape), _const_spec(m_tab.shape), _const_spec(s_tab.shape)],
        out_specs=out_spec,
        out_shape=out_shape,
        compiler_params=_cparams("arbitrary", "arbitrary"),
        name="hgrn2",
    )(p, p, p, p, lb_logits_rows, norm_rows, w_tab, m_tab, s_tab)


def _retention_kernel(q_ref, k_ref, v_ref, g_ref, cos_ref, sin_ref, lg_ref, nw_ref, y_ref, *, seq, hb):
    L = REC_CHUNK
    row = lax.broadcasted_iota(jnp.int32, (L, L), 0)
    col = lax.broadcasted_iota(jnp.int32, (L, L), 1)
    dist = (row - col).astype(F32)
    pos = lax.broadcasted_iota(jnp.int32, (L, HEAD_DIM), 0).astype(F32)

    def decays(h):
        lg = lg_ref[h]
        return (jnp.where(col <= row, jnp.exp(dist * lg[:, 0:1]), 0.0),
                jnp.exp((pos + 1.0) * lg),
                jnp.exp((L - 1.0 - pos) * lg),
                jnp.exp(L * lg))

    dec = [decays(h) for h in range(hb)]

    def rot(x, cs, sn):
        return x * cs + pltpu.roll(x, HEAD_DIM // 2, 1) * sn

    def body(c, carry):
        rows = pl.ds(pl.multiple_of(c * L, L), L)
        cs = cos_ref[rows, :]
        sn = sin_ref[rows, :]
        heads = range(hb)
        q = [rot(q_ref[0, h, rows, :], cs, sn) for h in heads]
        k = [rot(k_ref[0, h, rows, :], cs, sn) * (HEAD_DIM ** -0.5) for h in heads]
        v = [v_ref[0, h, rows, :].astype(BF16) for h in heads]
        s_raw = [_dot_tb(q[h].astype(BF16), k[h].astype(BF16)) for h in heads]
        inter = [_dot_tb((q[h] * dec[h][1]).astype(BF16), carry[h].astype(BF16)) for h in heads]
        intra = [_dot((s_raw[h] * dec[h][0]).astype(BF16), v[h]) for h in heads]
        for h in heads:
            y = _rms_lanes(inter[h] + intra[h]) * nw_ref[h] * _silu(g_ref[0, h, rows, :])
            y_ref[0, rows, h * HEAD_DIM:(h + 1) * HEAD_DIM] = y.astype(y_ref.dtype)
        return tuple(dec[h][3] * carry[h] + _dot_ta(v[h], (k[h] * dec[h][2]).astype(BF16)) for h in heads)

    lax.fori_loop(0, seq // L, body, tuple(jnp.zeros((HEAD_DIM, HEAD_DIM), F32) for _ in range(hb)),
                  unroll=REC_UNROLL)


def _retention(p, norm_rows):
    bsz, _, t, _ = p.shape
    hb = RET_HEADS_PER_STEP
    inv_freq = ROPE_BASE ** (-np.arange(0, HEAD_DIM, 2, dtype=np.float32) / HEAD_DIM)
    ang = jnp.arange(t, dtype=F32)[:, None] * jnp.asarray(inv_freq, F32)[None, :]
    cos, sin = jnp.cos(ang), jnp.sin(ang)
    cos2 = jnp.concatenate([cos, cos], axis=-1)
    sin2 = jnp.concatenate([-sin, sin], axis=-1)
    log_gamma = np.log1p(-(2.0 ** (-5.0 - np.arange(N_HEADS, dtype=np.float32)))).astype(np.float32)
    lg_rows = jnp.asarray(np.broadcast_to(log_gamma[:, None, None], (N_HEADS, 1, LANES)).copy())
    out_spec, out_shape = _mixer_out(bsz, t, hb)
    per_head = pl.BlockSpec((hb, 1, LANES), lambda bi, hi: (hi, 0, 0))
    return pl.pallas_call(
        functools.partial(_retention_kernel, seq=t, hb=hb),
        grid=(bsz, N_HEADS // hb),
        in_specs=[_head_spec(t, hb, 0), _head_spec(t, hb, N_HEADS), _head_spec(t, hb, 2 * N_HEADS),
                  _head_spec(t, hb, 3 * N_HEADS),
                  _const_spec((t, LANES)), _const_spec((t, LANES)), per_head, per_head],
        out_specs=out_spec,
        out_shape=out_shape,
        compiler_params=_cparams("arbitrary", "arbitrary"),
        name="retention",
    )(p, p, p, p, cos2, sin2, lg_rows, norm_rows)


def _sb_kernel(q_ref, k_ref, v_ref, su_ref, y_ref, z0_ref, z1_ref, lhs0_ref, lhs1_ref, w0_ref, w1_ref,
               sum0_ref, sum1_ref, carry_ref, acc_ref):
    z_bufs, lhs_bufs = (z0_ref, z1_ref), (lhs0_ref, lhs1_ref)
    w_bufs, sum_bufs = (w0_ref, w1_ref), (sum0_ref, sum1_ref)
    tq, tk, rs = SB_Q_TILE, SB_K_TILE, SB_STRIP
    n_strip = tq // rs
    ratio = tq // tk
    assert ratio == 2
    qi = pl.program_id(2)
    n_full = qi * ratio
    q2 = q_ref[0, 0] * (SB_HEAD_DIM ** -0.5 * LOG2_E)
    first = lax.broadcasted_iota(jnp.int32, (tq, LANES), 1) < SB_HEAD_DIM
    q_cat = jnp.concatenate([jnp.where(first, q2, 0.0), jnp.where(first, 0.0, q2)], axis=0).astype(BF16)
    row = lax.broadcasted_iota(jnp.int32, (rs, tk), 0)
    col = lax.broadcasted_iota(jnp.int32, (rs, tk), 1)

    def tile_of(idx):
        if isinstance(idx, int) and idx < ratio:
            return qi * ratio + (ratio - 1 - idx), (ratio - 1 - idx) * tk
        return n_full - 1 - (idx - ratio), None

    def rows_of(idx):
        return pl.ds(pl.multiple_of(tile_of(idx)[0] * tk, tk), tk)

    def live_rows(idx):
        lo = tile_of(idx)[1] or 0
        return [slice(0, 2 * tq)] if lo == 0 else [slice(lo, tq), slice(tq + lo, 2 * tq)]

    def gather(ref_or_val, blocks):
        parts = [ref_or_val[blk] for blk in blocks]
        return parts[0] if len(parts) == 1 else jnp.concatenate(parts, axis=0)

    def stage_a(idx, slot):
        blocks = live_rows(idx)
        n_live = sum(blk.stop - blk.start for blk in blocks)
        z_bufs[slot][0:n_live, :] = _dot_tb(gather(q_cat, blocks), k_ref[0, 0, rows_of(idx), :].astype(BF16))

    def stage_b(idx, slot):
        key_off = tile_of(idx)[1]
        blocks = live_rows(idx)
        n_live = sum(blk.stop - blk.start for blk in blocks)
        per_head = n_live // 2
        z_buf, lhs_buf, w_buf = z_bufs[slot], lhs_bufs[slot], w_bufs[slot]

        def visible(s):
            first_row = (tq - per_head) + (s * rs) % per_head
            return col + key_off < row + first_row

        for s in range(n_live // rs):
            strip = slice(s * rs, (s + 1) * rs)
            zs = z_buf[strip, :]
            sp = jnp.maximum(zs, 0.0) + jnp.log2(1.0 + jnp.exp2(-jnp.abs(zs)))
            if key_off is not None:
                sp = jnp.where(visible(s), sp, 0.0)
            hi = sp.astype(BF16)
            lhs_buf[strip, 0:tk] = hi
            lhs_buf[strip, tk:2 * tk] = (sp - hi.astype(F32)).astype(BF16)
        incl = _dot(lhs_buf[0:n_live, :], su_ref[...])
        for s in range(n_live // rs):
            strip = slice(s * rs, (s + 1) * rs)
            arg = z_buf[strip, :] - incl[strip]
            if key_off is not None:
                arg = jnp.where(visible(s), arg, NEG_BIG)
            w_buf[strip, :] = jnp.exp2(arg).astype(BF16)
        sum_bufs[slot][0:n_live, :] = jnp.broadcast_to(incl[:, 0:1], (n_live, LANES))

    def stage_c(idx, slot):
        blocks = live_rows(idx)
        n_live = sum(blk.stop - blk.start for blk in blocks)
        v2 = v_ref[0, 0, rows_of(idx), :].astype(BF16)
        carry = gather(carry_ref, blocks)
        part = jnp.exp2(-carry) * _dot(w_bufs[slot][0:n_live, :], v2)
        new_carry = carry + sum_bufs[slot][0:n_live, :]
        at = 0
        for blk in blocks:
            size = blk.stop - blk.start
            acc_ref[blk, :] += part[at:at + size]
            carry_ref[blk, :] = new_carry[at:at + size]
            at += size

    def finish():
        y_ref[0] = jnp.where(first, acc_ref[0:tq, :], acc_ref[tq:2 * tq, :]).astype(y_ref.dtype)

    carry_ref[...] = jnp.zeros((2 * tq, LANES), F32)
    acc_ref[...] = jnp.zeros((2 * tq, LANES), F32)
    stage_a(0, 0)
    stage_a(1, 1)
    stage_b(0, 0)

    @pl.when(qi == 0)
    def _():
        stage_b(1, 1)
        stage_c(0, 0)
        stage_c(1, 1)
        finish()

    @pl.when(qi > 0)
    def _():
        stage_a(2, 0)
        stage_b(1, 1)
        stage_c(0, 0)
        stage_a(3, 1)
        stage_b(2, 0)
        stage_c(1, 1)

        def two_steps(i, _):
            n = 2 + 2 * i
            stage_a(n + 2, 0)
            stage_b(n + 1, 1)
            stage_c(n, 0)
            stage_a(n + 3, 1)
            stage_b(n + 2, 0)
            stage_c(n + 1, 1)
            return 0

        lax.fori_loop(0, qi - 1, two_steps, 0)
        last = ratio + n_full - 1
        stage_b(last, 1)
        stage_c(last - 1, 0)
        stage_c(last, 1)
        finish()


def _stick_breaking(p):
    bsz, _, t, _ = p.shape
    tq, tk = SB_Q_TILE, SB_K_TILE
    n_pair = GROUP_W // LANES
    idx = np.arange(tk)
    from_s = (idx[:, None] >= idx[None, :]).astype(np.float32)
    su = jnp.asarray(np.concatenate([from_s, from_s], axis=0), BF16)
    return pl.pallas_call(
        _sb_kernel,
        grid=(bsz, n_pair, t // tq),
        in_specs=[pl.BlockSpec((1, 1, tq, LANES), lambda bi, hi, qi: (bi, 4 * N_HEADS + hi, qi, 0)),
                  pl.BlockSpec((1, 1, t, LANES), lambda bi, hi, qi: (bi, 4 * N_HEADS + n_pair + hi, 0, 0)),
                  pl.BlockSpec((1, 1, t, LANES), lambda bi, hi, qi: (bi, 4 * N_HEADS + 2 * n_pair + hi, 0, 0)),
                  _const_spec((2 * tk, tk))],
        out_specs=pl.BlockSpec((1, tq, LANES), lambda bi, hi, qi: (bi, qi, hi)),
        out_shape=jax.ShapeDtypeStruct((bsz, t, GROUP_W), BF16),
        scratch_shapes=(2 * [pltpu.VMEM((2 * tq, tk), F32)] + 2 * [pltpu.VMEM((2 * tq, 2 * tk), BF16)]
                        + 2 * [pltpu.VMEM((2 * tq, tk), BF16)] + 4 * [pltpu.VMEM((2 * tq, LANES), F32)]),
        compiler_params=_cparams("arbitrary", "arbitrary", "arbitrary"),
        name="stick_breaking",
    )(p, p, p, su)


def _ffn_kernel(*refs, tm, final):
    if final:
        (x_ref, y1_ref, y2_ref, mod_ref, nw_ref, wo_ref, wu_ref, cw_ref, cb_ref, wd_ref, nf_ref,
         o_ref, ext_ref, carry_ref, act_ref, xs_ref) = refs
    else:
        (x_ref, y1_ref, y2_ref, mod_ref, nw_ref, wo_ref, wu_ref, cw_ref, cb_ref, wd_ref,
         o_ref, ext_ref, carry_ref, act_ref, xs_ref) = refs
    fc = FF_CHUNK
    n_chunks = D_FF // fc

    @pl.when(pl.program_id(1) == 0)
    def _():
        carry_ref[...] = jnp.zeros(carry_ref.shape, F32)

    depth = tm // SUBLANES
    halo = FFN_CONV - 1
    mod = mod_ref[0]
    mix = _dot(jnp.concatenate([y1_ref[0], y2_ref[0]], axis=1), wo_ref[...])
    n_lane_tiles = D_MODEL // LANES

    def restage(val, start_of, stride):
        for ct in range(n_lane_tiles):
            xs_ref[ct] = val[:, ct * LANES:(ct + 1) * LANES]
        return jnp.concatenate(
            [jnp.concatenate([xs_ref[ct, pl.ds(start_of(g), SUBLANES, stride=stride), :]
                              for ct in range(n_lane_tiles)], axis=1) for g in range(depth)], axis=0)

    x1 = restage(x_ref[0] + mod[2:3] * mix, lambda g: g, depth)
    hb = _modulated_norm(x1, nw_ref[...], mod[3:4], mod[4:5]).astype(BF16)
    acc = jnp.zeros((tm, D_MODEL), F32)
    sublane0 = lax.broadcasted_iota(jnp.int32, (SUBLANES, 2 * fc), 0) == 0

    def pair(ref, rows, j):
        return jnp.concatenate([ref[rows, j * fc:(j + 1) * fc], ref[rows, D_FF + j * fc:D_FF + (j + 1) * fc]],
                               axis=1)

    def up_project(j):
        ext = ext_ref.at[j % 2]
        u = _dot(hb, pair(wu_ref, slice(None), j))
        ext[halo * SUBLANES:halo * SUBLANES + tm, :] = u
        tail = u[tm - halo * SUBLANES:tm]
        prev = carry_ref[j]
        for r in range(halo):
            grp = slice(r * SUBLANES, (r + 1) * SUBLANES)
            ext[grp, :] = jnp.where(sublane0, pltpu.roll(prev[grp], 1, 0), pltpu.roll(tail[grp], 1, 0))
        carry_ref[j] = tail

    def down_project(acc, ks):
        return acc + _dot(act_ref[:, ks], wd_ref[ks, :])

    up_project(0)
    pending = None
    for j in range(n_chunks):
        if j + 1 < n_chunks:
            up_project(j + 1)
        if pending is not None:
            acc = down_project(acc, pending)
            pending = None
        ext = ext_ref.at[j % 2]
        u = pair(cb_ref, slice(None), j)
        for i in range(FFN_CONV):
            u = u + pair(cw_ref, slice(i, i + 1), j) * ext[i * SUBLANES:i * SUBLANES + tm, :]
        act_ref[:, j * fc:(j + 1) * fc] = (u[:, :fc] * _silu(u[:, fc:])).astype(BF16)
        if (j + 1) % FF_DOWN_GROUP == 0 or j + 1 == n_chunks:
            pending = slice((j // FF_DOWN_GROUP) * FF_DOWN_GROUP * fc, (j + 1) * fc)
    x2 = x1 + mod[5:6] * down_project(acc, pending)
    if final:
        x2 = _rms_lanes(x2) * nf_ref[...]
    o_ref[0] = restage(x2, lambda g: ((g * SUBLANES) % depth) * SUBLANES + (g * SUBLANES) // depth, SUBLANES)


def _ffn(x, y1, y2, mod_l, nw, w_out, w_up, conv_w, conv_b, w_down, norm_final=None):
    bsz, t, d = x.shape
    tm = min(TOK_TILE, t)
    final = norm_final is not None
    in_specs = [pl.BlockSpec((1, tm, d), lambda bi, ti: (bi, ti, 0)),
                pl.BlockSpec((1, tm, GROUP_W), lambda bi, ti: (bi, ti, 0)),
                pl.BlockSpec((1, tm, GROUP_W), lambda bi, ti: (bi, ti, 0)),
                pl.BlockSpec((1, 6, d), lambda bi, ti: (bi, 0, 0)),
                _const_spec((1, d)),
                _const_spec(w_out.shape), _const_spec(w_up.shape),
                _const_spec(conv_w.shape), _const_spec(conv_b.shape),
                _const_spec(w_down.shape)]
    args = [x, y1, y2, mod_l, nw, w_out, w_up, conv_w, conv_b, w_down]
    if final:
        in_specs.append(_const_spec((1, d)))
        args.append(norm_final)
    return pl.pallas_call(
        functools.partial(_ffn_kernel, tm=tm, final=final),
        grid=(bsz, t // tm),
        in_specs=in_specs,
        out_specs=pl.BlockSpec((1, tm, d), lambda bi, ti: (bi, ti, 0)),
        out_shape=jax.ShapeDtypeStruct((bsz, t, d), F32),
        scratch_shapes=[pltpu.VMEM((2, tm + (FFN_CONV - 1) * SUBLANES, 2 * FF_CHUNK), F32),
                        pltpu.VMEM((D_FF // FF_CHUNK, (FFN_CONV - 1) * SUBLANES, 2 * FF_CHUNK), F32),
                        pltpu.VMEM((tm, D_FF), BF16),
                        pltpu.VMEM((D_MODEL // LANES, tm, LANES), F32)],
        compiler_params=_cparams("arbitrary", "arbitrary"),
        name="outproj_ffn_final" if final else "outproj_ffn",
    )(*args)


def _head_rows(vec):
    return vec.reshape(-1, 1, LANES)


def kernel(x, c, w_ada, b_ada, norm_mix, norm_ffn, norm_final, w_in_even, b_in_even, conv_qk_w,
           conv_qk_b, mlstm_f_bias, mlstm_norm, hgrn_lb_logits, hgrn_norm, w_in_odd, b_in_odd,
           ret_norm, w_out, ffn_up, ffn_conv_w, ffn_conv_b, ffn_down):
    bsz, t, d = x.shape
    assert d == D_MODEL and t % SB_Q_TILE == 0 and t % REC_CHUNK == 0
    mod = _ada(c, w_ada, b_ada).reshape(DEPTH, bsz, 6, d)
    idx = np.arange(REC_CHUNK)
    tri = jnp.asarray((idx[None, :] <= idx[:, None]).astype(np.float32), BF16)
    gate0 = 4 * GROUP_W
    n_even = hgrn_lb_logits.shape[0]
    lb_rows = hgrn_lb_logits.astype(F32).reshape(n_even, 1, GROUP_W)

    for layer in range(DEPTH):
        mod_l = mod[layer]
        nw_mix = norm_mix[layer][None, :]
        if layer % 2 == 0:
            e = layer // 2
            w, b = w_in_even[e], b_in_even[e]
            w_main = jnp.concatenate([w[:, :gate0], w[:, gate0 + 2 * N_HEADS:]], axis=1).astype(BF16)
            b_main = jnp.concatenate([b[:gate0], b[gate0 + 2 * N_HEADS:]])[None, :]
            pad = LANES - 2 * N_HEADS
            gw = jnp.pad(w[:, gate0:gate0 + 2 * N_HEADS], ((0, 0), (0, pad))).astype(BF16)
            gb = jnp.pad(b[gate0:gate0 + 2 * N_HEADS], (0, pad))[None, :]
            p, gates = _inproj(x, mod_l, nw_mix, w_main, b_main, True, gw, gb,
                               conv_qk_w[e], conv_qk_b[e][None, :])
            fb_row = jnp.pad(mlstm_f_bias[e], (N_HEADS, pad))[None, :]
            y1 = _mlstm(p, gates, fb_row, _head_rows(mlstm_norm[e]), tri)
            y2 = _hgrn(p, lb_rows, _head_rows(hgrn_norm[e]), e)
        else:
            o = layer // 2
            (p,) = _inproj(x, mod_l, nw_mix, w_in_odd[o].astype(BF16), b_in_odd[o][None, :], False)
            y1 = _retention(p, _head_rows(ret_norm[o]))
            y2 = _stick_breaking(p)
        x = _ffn(x, y1, y2, mod_l, norm_ffn[layer][None, :], w_out[layer].astype(BF16),
                 ffn_up[layer].astype(BF16), ffn_conv_w[layer], ffn_conv_b[layer][None, :],
                 ffn_down[layer].astype(BF16),
                 norm_final[None, :] if layer == DEPTH - 1 else None)
    return x
```

```python
import functools

import numpy as np
import jax
import jax.numpy as jnp
from jax import lax
from jax.experimental import pallas as pl
from jax.experimental.pallas import tpu as pltpu

F32 = jnp.float32
BF16 = jnp.bfloat16

D_MODEL = 1024
DEPTH = 4
HEAD_DIM = 128
GROUP_W = D_MODEL // 2
N_HEADS = GROUP_W // HEAD_DIM
SB_HEAD_DIM = 64
CONV_A = 4
FFN_CONV = 3
D_FF = 2816
ROPE_BASE = 10000.0
EPS = 1e-6

LANES = 128
SUBLANES = 8
REC_CHUNK = 128
REC_UNROLL = 2
MLSTM_HEADS_PER_STEP = 4
MLSTM_GATE_UNROLL = 4
HGRN_HEADS_PER_STEP = 4
RET_HEADS_PER_STEP = 4
SB_Q_TILE = 512
SB_K_TILE = 256
SB_STRIP = 16
TOK_TILE = 512
NORM_STRIP = 64
FF_CHUNK = 256
FF_DOWN_GROUP = 4
VMEM_LIMIT = 60 * 1024 * 1024
NEG_BIG = -1e30
LOG2_E = 1.4426950408889634


def _dot(a, b):
    return jnp.dot(a, b, preferred_element_type=F32)


def _dot_tb(a, b):
    return lax.dot_general(a, b, (((1,), (1,)), ((), ())), preferred_element_type=F32)


def _dot_ta(a, b):
    return lax.dot_general(a, b, (((0,), (0,)), ((), ())), preferred_element_type=F32)


def _split3(x):
    hi = x.astype(BF16)
    r1 = x - hi.astype(F32)
    mid = r1.astype(BF16)
    lo = (r1 - mid.astype(F32)).astype(BF16)
    return hi, mid, lo


def _dot_lhs01(w01, x, terms):
    parts = _split3(x)[:terms]
    out = _dot(w01, parts[0])
    for part in parts[1:]:
        out = out + _dot(w01, part)
    return out


def _sigmoid(x):
    return jax.nn.sigmoid(x)


def _silu(x):
    return x * jax.nn.sigmoid(x)


def _log_sigmoid(x):
    return jnp.minimum(x, 0.0) - jnp.log1p(jnp.exp(-jnp.abs(x)))


def _rms_lanes(x):
    return x * lax.rsqrt(jnp.mean(x * x, axis=-1, keepdims=True) + EPS)


def _cparams(*sem):
    return pltpu.CompilerParams(dimension_semantics=sem, vmem_limit_bytes=VMEM_LIMIT)


def _const_spec(shape):
    nd = len(shape)
    return pl.BlockSpec(shape, lambda *_: (0,) * nd, pipeline_mode=pl.Buffered(1))


def _ada_kernel(c_ref, w_ref, b_ref, o_ref):
    c = c_ref[...]
    ada = jnp.dot(_silu(c), w_ref[...], preferred_element_type=F32,
                  precision=lax.Precision.HIGHEST)
    for layer in range(DEPTH):
        o_ref[layer] = ada + b_ref[layer:layer + 1, :]


def _ada(c, w_ada, b_ada):
    bsz, d = c.shape
    n = w_ada.shape[1]
    tn = 1024
    return pl.pallas_call(
        _ada_kernel,
        grid=(n // tn,),
        in_specs=[pl.BlockSpec((bsz, d), lambda j: (0, 0)),
                  pl.BlockSpec((d, tn), lambda j: (0, j)),
                  pl.BlockSpec((DEPTH, tn), lambda j: (0, j))],
        out_specs=pl.BlockSpec((DEPTH, bsz, tn), lambda j: (0, 0, j)),
        out_shape=jax.ShapeDtypeStruct((DEPTH, bsz, n), F32),
        compiler_params=_cparams("arbitrary"),
        name="ada_mod",
    )(c, w_ada, b_ada)


def _modulated_norm(x, nw, shift, scale):
    return _rms_lanes(x) * nw * (1.0 + scale) + shift


def _modulated_norm_bf16(strips, nw, shift, scale):
    gain = nw * (1.0 + scale)
    return jnp.concatenate([(_rms_lanes(s) * gain + shift).astype(BF16) for s in strips], axis=0)


def _inproj_kernel(*refs, even, n_cols, tm):
    if even:
        (x_ref, mod_ref, nw_ref, w_ref, b_ref, wg_ref, bg_ref, cw_ref, cb_ref,
         p_ref, g_ref, ext_ref) = refs
    else:
        x_ref, mod_ref, nw_ref, w_ref, b_ref, p_ref = refs
    mod = mod_ref[0]
    hb = _modulated_norm_bf16([x_ref[0, r:r + NORM_STRIP, :] for r in range(0, tm, NORM_STRIP)],
                              nw_ref[...], mod[0:1], mod[1:2])
    col_chunk = 4 * LANES
    conv_cols = 2 * GROUP_W

    if even:
        @pl.when(pl.program_id(1) == 0)
        def _():
            ext_ref[0:SUBLANES, :] = jnp.zeros((SUBLANES, conv_cols), F32)

    def project(j):
        cols = slice(j * col_chunk, (j + 1) * col_chunk)
        return _dot(hb, w_ref[:, cols]) + b_ref[:, cols]

    def emit(j, res):
        for i in range(col_chunk // LANES):
            p_ref[0, j * (col_chunk // LANES) + i] = res[:, i * LANES:(i + 1) * LANES]

    def conv_epilogue(j):
        cols = slice(j * col_chunk, (j + 1) * col_chunk)
        acc = cb_ref[:, cols]
        for i in range(CONV_A):
            off = SUBLANES - (CONV_A - 1) + i
            acc = acc + cw_ref[i:i + 1, cols] * ext_ref[off:off + tm, cols]
        ext_ref[0:SUBLANES, cols] = ext_ref[tm:tm + SUBLANES, cols]
        emit(j, _silu(acc))

    n_chunks = n_cols // col_chunk
    n_conv = conv_cols // col_chunk if even else 0
    for j in range(n_conv):
        ext_ref[SUBLANES:SUBLANES + tm, j * col_chunk:(j + 1) * col_chunk] = project(j)
    waiting = list(range(n_conv))
    for j in range(n_conv, n_chunks):
        emit(j, project(j))
        if waiting:
            conv_epilogue(waiting.pop(0))
    while waiting:
        conv_epilogue(waiting.pop(0))

    if even:
        g_ref[0] = _dot(hb, wg_ref[...]) + bg_ref[...]


def _inproj(x, mod_l, nw, w, b, even, gate_w=None, gate_b=None, conv_w=None, conv_b=None):
    bsz, t, d = x.shape
    n_cols = w.shape[1]
    nblk = n_cols // LANES
    tm = min(TOK_TILE, t)
    in_specs = [pl.BlockSpec((1, tm, d), lambda bi, ti: (bi, ti, 0)),
                pl.BlockSpec((1, 6, d), lambda bi, ti: (bi, 0, 0)),
                _const_spec((1, d)),
                _const_spec((d, n_cols)),
                _const_spec((1, n_cols))]
    args = [x, mod_l, nw, w, b]
    out_specs = [pl.BlockSpec((1, nblk, tm, LANES), lambda bi, ti: (bi, 0, ti, 0))]
    out_shape = [jax.ShapeDtypeStruct((bsz, nblk, t, LANES), F32)]
    scratch = []
    if even:
        in_specs += [_const_spec((d, LANES)), _const_spec((1, LANES)),
                     _const_spec((CONV_A, 2 * GROUP_W)), _const_spec((1, 2 * GROUP_W))]
        args += [gate_w, gate_b, conv_w, conv_b]
        out_specs.append(pl.BlockSpec((1, tm, LANES), lambda bi, ti: (bi, ti, 0)))
        out_shape.append(jax.ShapeDtypeStruct((bsz, t, LANES), F32))
        scratch.append(pltpu.VMEM((tm + SUBLANES, 2 * GROUP_W), F32))
    return pl.pallas_call(
        functools.partial(_inproj_kernel, even=even, n_cols=n_cols, tm=tm),
        grid=(bsz, t // tm),
        in_specs=in_specs,
        out_specs=out_specs,
        out_shape=out_shape,
        scratch_shapes=scratch,
        compiler_params=_cparams("arbitrary", "arbitrary"),
        name="inproj_even" if even else "inproj_odd",
    )(*args)


def _mlstm_kernel(q_ref, k_ref, v_ref, o_ref, g_ref, fb_ref, nw_ref, tri_ref, y_ref,
                  arow_ref, cum_ref, gmax_ref, m_ref, *, seq, hb):
    L = REC_CHUNK
    assert L == LANES
    n_chunks = seq // L
    row = lax.broadcasted_iota(jnp.int32, (L, L), 0)
    col = lax.broadcasted_iota(jnp.int32, (L, L), 1)
    causal = col <= row
    ones = jnp.ones((L, HEAD_DIM), BF16)

    def gate_pass(c, _):
        rows = pl.ds(pl.multiple_of(c * L, L), L)
        gates = g_ref[0, rows, :] + fb_ref[...]
        cum_t = _dot_lhs01(tri_ref[...], _log_sigmoid(gates), 3)
        a_t = (gates - pltpu.roll(cum_t, LANES - N_HEADS, 1)).T
        arow_ref[c] = a_t[0:SUBLANES, :]
        for h in range(hb):
            cum_ref[h, rows, :] = jnp.broadcast_to(cum_t[:, N_HEADS + h:N_HEADS + h + 1], (L, LANES))
            a_b = jnp.broadcast_to(a_t[h:h + 1, :], (L, L))
            run_max = jnp.max(jnp.where(causal, a_b, NEG_BIG), axis=1, keepdims=True)
            gmax_ref[h, rows, :] = jnp.broadcast_to(run_max, (L, LANES))
        return 0

    lax.fori_loop(0, n_chunks, gate_pass, 0, unroll=MLSTM_GATE_UNROLL)

    for h in range(hb):
        m = jnp.zeros((1, LANES), F32)
        for c in range(n_chunks):
            m_ref[h * n_chunks + c:h * n_chunks + c + 1, :] = m
            last = slice(c * L + L - 1, c * L + L)
            m = cum_ref[h, last, :] + jnp.maximum(gmax_ref[h, last, :], m)

    def twice(x):
        return jnp.concatenate([x, x], axis=1)

    def body(c, carry):
        rows = pl.ds(pl.multiple_of(c * L, L), L)
        heads = range(hb)
        k = [k_ref[0, h, rows, :] * (HEAD_DIM ** -0.5) for h in heads]
        qb = [q_ref[0, h, rows, :].astype(BF16) for h in heads]
        v_aug = [jnp.concatenate([v_ref[0, h, rows, :].astype(BF16), ones], axis=1) for h in heads]
        k_t = [k[h].T for h in heads]
        s_raw = [_dot_tb(qb[h], k[h].astype(BF16)) for h in heads]
        inter = [_dot(qb[h], carry[h].astype(BF16)) for h in heads]
        a_rows = arow_ref[c]
        m_prev, g, s_dec, kw_t, decay = [], [], [], [], []
        for h in heads:
            m_prev.append(m_ref[pl.ds(h * n_chunks + c, 1), :])
            g.append(jnp.maximum(gmax_ref[h, rows, :], m_prev[h]))
            a_row = a_rows[h:h + 1, :]
            a_b = jnp.broadcast_to(a_row, (L, L))
            s_dec.append((s_raw[h] * jnp.exp(jnp.where(causal, a_b - g[h], NEG_BIG))).astype(BF16))
            g_last = g[h][L - 1:L, :]
            kw_t.append((k_t[h] * jnp.exp(a_row - g_last)).astype(BF16))
            decay.append(jnp.exp(m_prev[h] - g_last))
        intra = [_dot(s_dec[h], v_aug[h]) for h in heads]
        new = tuple(twice(decay[h]) * carry[h] + _dot(kw_t[h], v_aug[h]) for h in heads)
        for h in heads:
            r = intra[h] + twice(jnp.exp(m_prev[h] - g[h])) * inter[h]
            num = r[:, :HEAD_DIM]
            den = r[:, HEAD_DIM:]
            hid = num / jnp.maximum(jnp.abs(den), jnp.exp(-(cum_ref[h, rows, :] + g[h])))
            y = _rms_lanes(hid) * nw_ref[h] * _sigmoid(o_ref[0, h, rows, :])
            y_ref[0, rows, h * HEAD_DIM:(h + 1) * HEAD_DIM] = y.astype(y_ref.dtype)
        return new

    lax.fori_loop(0, n_chunks, body, tuple(jnp.zeros((HEAD_DIM, 2 * HEAD_DIM), F32) for _ in range(hb)),
                  unroll=REC_UNROLL)


def _head_spec(t, hb, off):
    assert off % hb == 0
    return pl.BlockSpec((1, hb, t, LANES), lambda bi, hi: (bi, off // hb + hi, 0, 0))


def _mixer_out(bsz, t, hb):
    return (pl.BlockSpec((1, t, hb * LANES), lambda bi, hi: (bi, 0, hi)),
            jax.ShapeDtypeStruct((bsz, t, GROUP_W), BF16))


def _mlstm(p, gates, f_bias_row, norm_rows, tri):
    bsz, _, t, _ = p.shape
    hb = MLSTM_HEADS_PER_STEP
    assert hb == N_HEADS
    out_spec, out_shape = _mixer_out(bsz, t, hb)
    return pl.pallas_call(
        functools.partial(_mlstm_kernel, seq=t, hb=hb),
        grid=(bsz, N_HEADS // hb),
        in_specs=[_head_spec(t, hb, 0), _head_spec(t, hb, N_HEADS), _head_spec(t, hb, 2 * N_HEADS),
                  _head_spec(t, hb, 3 * N_HEADS),
                  pl.BlockSpec((1, t, LANES), lambda bi, hi: (bi, 0, 0)),
                  _const_spec((1, LANES)),
                  pl.BlockSpec((hb, 1, LANES), lambda bi, hi: (hi, 0, 0)),
                  _const_spec((REC_CHUNK, REC_CHUNK))],
        out_specs=out_spec,
        out_shape=out_shape,
        scratch_shapes=[pltpu.VMEM((t // REC_CHUNK, SUBLANES, REC_CHUNK), F32),
                        pltpu.VMEM((hb, t, LANES), F32), pltpu.VMEM((hb, t, LANES), F32),
                        pltpu.VMEM((hb * (t // REC_CHUNK), LANES), F32)],
        compiler_params=_cparams("arbitrary", "arbitrary"),
        name="mlstm",
    )(p, p, p, p, gates, f_bias_row, norm_rows, tri)


def _hgrn_tables():
    L = REC_CHUNK
    t = np.arange(L)
    blocks = [t[None, :] <= t[:, None],
              t[None, :] > t[:, None]]
    masks, signs = [], []
    lev = 0
    while (1 << lev) < L:
        h = 1 << lev
        blk = t // (2 * h)
        upper = (t % (2 * h)) >= h
        mid = blk * 2 * h + h
        eq = upper[:, None] & (t[None, :] >= mid[:, None]) & (t[None, :] <= t[:, None])
        ek = (~upper)[:, None] & (t[None, :] > t[:, None]) & (t[None, :] < mid[:, None])
        if h < SUBLANES:
            blocks.append(eq | ek)
        else:
            signs.append(np.where(upper, 1.0, -1.0))
        masks.append((blk[:, None] == blk[None, :]) & upper[:, None] & (~upper)[None, :])
        lev += 1
    masks.append(t[:, None] == t[None, :])
    w = np.concatenate(blocks, axis=0).astype(np.float32)
    m = np.stack(masks).astype(np.float32)
    s = np.broadcast_to(np.stack(signs)[:, :, None], (len(signs), L, LANES)).astype(np.float32)
    return w, m, s, lev


def _hgrn_kernel(q_ref, f_ref, i_ref, g_ref, lbl_ref, nw_ref, w_ref, m_ref, s_ref, y_ref,
                 *, seq, n_lev, layer_e, hb):
    L = REC_CHUNK
    n_fine = min(n_lev, SUBLANES.bit_length() - 1)

    def lower_bound(h):
        sm = jax.nn.softmax(lbl_ref[:, 0, h * HEAD_DIM:(h + 1) * HEAD_DIM], axis=0)
        lb = jnp.zeros((1, HEAD_DIM), F32)
        for r in range(1, layer_e + 1):
            lb = lb + sm[r:r + 1, :]
        return lb

    lbs = [lower_bound(h) for h in range(hb)]

    def gates(h, rows):
        if layer_e == 0:
            z = f_ref[0, h, rows, :]
            return _log_sigmoid(z), _sigmoid(-z)
        lb = lbs[h]
        lb_pos = lb > 0.0
        log_lb = jnp.log(jnp.where(lb_pos, lb, 1.0))
        z = f_ref[0, h, rows, :]
        ls = jnp.log1p(-lb) + _log_sigmoid(z)
        lae = jnp.maximum(log_lb, ls) + jnp.log1p(jnp.exp(-jnp.abs(log_lb - ls)))
        return jnp.where(lb_pos, lae, ls), (1.0 - lb) * _sigmoid(-z)

    def body(c, carry):
        rows = pl.ds(pl.multiple_of(c * L, L), L)
        log_f, k, q, v, raw, inter, scores, intra, dec, new_state = ({} for _ in range(10))

        def exponents(h):
            log_f[h], k[h] = gates(h, rows)
            raw[h] = _dot_lhs01(w_ref[...], log_f[h], 2)

        def level_scores(h):
            dec[h] = jnp.exp(raw[h])
            q[h] = _silu(q_ref[0, h, rows, :])
            v[h] = i_ref[0, h, rows, :].astype(BF16)
            inter[h] = _dot_tb((q[h] * dec[h][0:L]).astype(BF16), carry[h].astype(BF16))
            parts = [_dot_tb(q[h].astype(BF16), k[h].astype(BF16))]
            cum = raw[h][0:L]
            for lev in range(n_lev):
                half = 1 << lev
                if half < SUBLANES:
                    fac = dec[h][(2 + lev) * L:(3 + lev) * L]
                else:
                    ref = jnp.concatenate(
                        [jnp.broadcast_to(cum[b * 2 * half + half - 1:b * 2 * half + half, :], (2 * half, HEAD_DIM))
                         for b in range(L // (2 * half))], axis=0)
                    fac = jnp.exp((cum - ref) * s_ref[lev - n_fine])
                parts.append(_dot_tb((q[h] * fac).astype(BF16), (k[h] * fac).astype(BF16)))
            scores[h] = parts

        def mix_values(h):
            acc = scores[h][0] * m_ref[n_lev]
            for lev in range(n_lev):
                acc = acc + scores[h][1 + lev] * m_ref[lev]
            intra[h] = _dot(acc.astype(BF16), v[h])

        def emit(h):
            y = _rms_lanes(inter[h] + intra[h]) * nw_ref[h] * _silu(g_ref[0, h, rows, :])
            y_ref[0, rows, h * HEAD_DIM:(h + 1) * HEAD_DIM] = y.astype(y_ref.dtype)
            new_state[h] = (dec[h][L - 1:L, :] * carry[h]
                            + _dot_ta(v[h], (k[h] * dec[h][L:2 * L]).astype(BF16)))

        stages = (exponents, level_scores, mix_values, emit)
        for step in range(hb + len(stages) - 1):
            for lag, stage in enumerate(stages):
                if 0 <= step - lag < hb:
                    stage(step - lag)
        return tuple(new_state[h] for h in range(hb))

    lax.fori_loop(0, seq // L, body, tuple(jnp.zeros((HEAD_DIM, HEAD_DIM), F32) for _ in range(hb)),
                  unroll=REC_UNROLL)


def _hgrn(p, lb_logits_rows, norm_rows, layer_e):
    bsz, _, t, _ = p.shape
    hb = HGRN_HEADS_PER_STEP
    w_np, m_np, s_np, n_lev = _hgrn_tables()
    w_tab = jnp.asarray(w_np, BF16)
    m_tab = jnp.asarray(m_np, F32)
    s_tab = jnp.asarray(s_np, F32)
    n_even = lb_logits_rows.shape[0]
    out_spec, out_shape = _mixer_out(bsz, t, hb)
    return pl.pallas_call(
        functools.partial(_hgrn_kernel, seq=t, n_lev=n_lev, layer_e=layer_e, hb=hb),
        grid=(bsz, N_HEADS // hb),
        in_specs=[_head_spec(t, hb, 4 * N_HEADS), _head_spec(t, hb, 5 * N_HEADS),
                  _head_spec(t, hb, 6 * N_HEADS), _head_spec(t, hb, 7 * N_HEADS),
                  pl.BlockSpec((n_even, 1, hb * LANES), lambda bi, hi: (0, 0, hi)),
                  pl.BlockSpec((hb, 1, LANES), lambda bi, hi: (hi, 0, 0)),
                  _const_spec(w_tab.shape), _const_spec(m_tab.shape), _const_spec(s_tab.shape)],
        out_specs=out_spec,
        out_shape=out_shape,
        compiler_params=_cparams("arbitrary", "arbitrary"),
        name="hgrn2",
    )(p, p, p, p, lb_logits_rows, norm_rows, w_tab, m_tab, s_tab)


def _retention_kernel(q_ref, k_ref, v_ref, g_ref, cos_ref, sin_ref, lg_ref, nw_ref, y_ref, *, seq, hb):
    L = REC_CHUNK
    row = lax.broadcasted_iota(jnp.int32, (L, L), 0)
    col = lax.broadcasted_iota(jnp.int32, (L, L), 1)
    dist = (row - col).astype(F32)
    pos = lax.broadcasted_iota(jnp.int32, (L, HEAD_DIM), 0).astype(F32)

    def decays(h):
        lg = lg_ref[h]
        return (jnp.where(col <= row, jnp.exp(dist * lg[:, 0:1]), 0.0),
                jnp.exp((pos + 1.0) * lg),
                jnp.exp((L - 1.0 - pos) * lg),
                jnp.exp(L * lg))

    dec = [decays(h) for h in range(hb)]

    def rot(x, cs, sn):
        return x * cs + pltpu.roll(x, HEAD_DIM // 2, 1) * sn

    def body(c, carry):
        rows = pl.ds(pl.multiple_of(c * L, L), L)
        cs = cos_ref[rows, :]
        sn = sin_ref[rows, :]
        heads = range(hb)
        q = [rot(q_ref[0, h, rows, :], cs, sn) for h in heads]
        k = [rot(k_ref[0, h, rows, :], cs, sn) * (HEAD_DIM ** -0.5) for h in heads]
        v = [v_ref[0, h, rows, :].astype(BF16) for h in heads]
        s_raw = [_dot_tb(q[h].astype(BF16), k[h].astype(BF16)) for h in heads]
        inter = [_dot_tb((q[h] * dec[h][1]).astype(BF16), carry[h].astype(BF16)) for h in heads]
        intra = [_dot((s_raw[h] * dec[h][0]).astype(BF16), v[h]) for h in heads]
        for h in heads:
            y = _rms_lanes(inter[h] + intra[h]) * nw_ref[h] * _silu(g_ref[0, h, rows, :])
            y_ref[0, rows, h * HEAD_DIM:(h + 1) * HEAD_DIM] = y.astype(y_ref.dtype)
        return tuple(dec[h][3] * carry[h] + _dot_ta(v[h], (k[h] * dec[h][2]).astype(BF16)) for h in heads)

    lax.fori_loop(0, seq // L, body, tuple(jnp.zeros((HEAD_DIM, HEAD_DIM), F32) for _ in range(hb)),
                  unroll=REC_UNROLL)


def _retention(p, norm_rows):
    bsz, _, t, _ = p.shape
    hb = RET_HEADS_PER_STEP
    inv_freq = ROPE_BASE ** (-np.arange(0, HEAD_DIM, 2, dtype=np.float32) / HEAD_DIM)
    ang = jnp.arange(t, dtype=F32)[:, None] * jnp.asarray(inv_freq, F32)[None, :]
    cos, sin = jnp.cos(ang), jnp.sin(ang)
    cos2 = jnp.concatenate([cos, cos], axis=-1)
    sin2 = jnp.concatenate([-sin, sin], axis=-1)
    log_gamma = np.log1p(-(2.0 ** (-5.0 - np.arange(N_HEADS, dtype=np.float32)))).astype(np.float32)
    lg_rows = jnp.asarray(np.broadcast_to(log_gamma[:, None, None], (N_HEADS, 1, LANES)).copy())
    out_spec, out_shape = _mixer_out(bsz, t, hb)
    per_head = pl.BlockSpec((hb, 1, LANES), lambda bi, hi: (hi, 0, 0))
    return pl.pallas_call(
        functools.partial(_retention_kernel, seq=t, hb=hb),
        grid=(bsz, N_HEADS // hb),
        in_specs=[_head_spec(t, hb, 0), _head_spec(t, hb, N_HEADS), _head_spec(t, hb, 2 * N_HEADS),
                  _head_spec(t, hb, 3 * N_HEADS),
                  _const_spec((t, LANES)), _const_spec((t, LANES)), per_head, per_head],
        out_specs=out_spec,
        out_shape=out_shape,
        compiler_params=_cparams("arbitrary", "arbitrary"),
        name="retention",
    )(p, p, p, p, cos2, sin2, lg_rows, norm_rows)


def _sb_kernel(q_ref, k_ref, v_ref, su_ref, y_ref, z0_ref, z1_ref, lhs0_ref, lhs1_ref, w0_ref, w1_ref,
               sum0_ref, sum1_ref, carry_ref, acc_ref):
    z_bufs, lhs_bufs = (z0_ref, z1_ref), (lhs0_ref, lhs1_ref)
    w_bufs, sum_bufs = (w0_ref, w1_ref), (sum0_ref, sum1_ref)
    tq, tk, rs = SB_Q_TILE, SB_K_TILE, SB_STRIP
    n_strip = tq // rs
    ratio = tq // tk
    assert ratio == 2
    qi = pl.program_id(2)
    n_full = qi * ratio
    q2 = q_ref[0, 0] * (SB_HEAD_DIM ** -0.5 * LOG2_E)
    first = lax.broadcasted_iota(jnp.int32, (tq, LANES), 1) < SB_HEAD_DIM
    q_cat = jnp.concatenate([jnp.where(first, q2, 0.0), jnp.where(first, 0.0, q2)], axis=0).astype(BF16)
    row = lax.broadcasted_iota(jnp.int32, (rs, tk), 0)
    col = lax.broadcasted_iota(jnp.int32, (rs, tk), 1)

    def tile_of(idx):
        if isinstance(idx, int) and idx < ratio:
            return qi * ratio + (ratio - 1 - idx), (ratio - 1 - idx) * tk
        return n_full - 1 - (idx - ratio), None

    def rows_of(idx):
        return pl.ds(pl.multiple_of(tile_of(idx)[0] * tk, tk), tk)

    def live_rows(idx):
        lo = tile_of(idx)[1] or 0
        return [slice(0, 2 * tq)] if lo == 0 else [slice(lo, tq), slice(tq + lo, 2 * tq)]

    def gather(ref_or_val, blocks):
        parts = [ref_or_val[blk] for blk in blocks]
        return parts[0] if len(parts) == 1 else jnp.concatenate(parts, axis=0)

    def stage_a(idx, slot):
        blocks = live_rows(idx)
        n_live = sum(blk.stop - blk.start for blk in blocks)
        z_bufs[slot][0:n_live, :] = _dot_tb(gather(q_cat, blocks), k_ref[0, 0, rows_of(idx), :].astype(BF16))

    def stage_b(idx, slot):
        key_off = tile_of(idx)[1]
        blocks = live_rows(idx)
        n_live = sum(blk.stop - blk.start for blk in blocks)
        per_head = n_live // 2
        z_buf, lhs_buf, w_buf = z_bufs[slot], lhs_bufs[slot], w_bufs[slot]

        def visible(s):
            first_row = (tq - per_head) + (s * rs) % per_head
            return col + key_off < row + first_row

        for s in range(n_live // rs):
            strip = slice(s * rs, (s + 1) * rs)
            zs = z_buf[strip, :]
            sp = jnp.maximum(zs, 0.0) + jnp.log2(1.0 + jnp.exp2(-jnp.abs(zs)))
            if key_off is not None:
                sp = jnp.where(visible(s), sp, 0.0)
            hi = sp.astype(BF16)
            lhs_buf[strip, 0:tk] = hi
            lhs_buf[strip, tk:2 * tk] = (sp - hi.astype(F32)).astype(BF16)
        incl = _dot(lhs_buf[0:n_live, :], su_ref[...])
        for s in range(n_live // rs):
            strip = slice(s * rs, (s + 1) * rs)
            arg = z_buf[strip, :] - incl[strip]
            if key_off is not None:
                arg = jnp.where(visible(s), arg, NEG_BIG)
            w_buf[strip, :] = jnp.exp2(arg).astype(BF16)
        sum_bufs[slot][0:n_live, :] = jnp.broadcast_to(incl[:, 0:1], (n_live, LANES))

    def stage_c(idx, slot):
        blocks = live_rows(idx)
        n_live = sum(blk.stop - blk.start for blk in blocks)
        v2 = v_ref[0, 0, rows_of(idx), :].astype(BF16)
        carry = gather(carry_ref, blocks)
        part = jnp.exp2(-carry) * _dot(w_bufs[slot][0:n_live, :], v2)
        new_carry = carry + sum_bufs[slot][0:n_live, :]
        at = 0
        for blk in blocks:
            size = blk.stop - blk.start
            acc_ref[blk, :] += part[at:at + size]
            carry_ref[blk, :] = new_carry[at:at + size]
            at += size

    def finish():
        y_ref[0] = jnp.where(first, acc_ref[0:tq, :], acc_ref[tq:2 * tq, :]).astype(y_ref.dtype)

    carry_ref[...] = jnp.zeros((2 * tq, LANES), F32)
    acc_ref[...] = jnp.zeros((2 * tq, LANES), F32)
    stage_a(0, 0)
    stage_a(1, 1)
    stage_b(0, 0)

    @pl.when(qi == 0)
    def _():
        stage_b(1, 1)
        stage_c(0, 0)
        stage_c(1, 1)
        finish()

    @pl.when(qi > 0)
    def _():
        stage_a(2, 0)
        stage_b(1, 1)
        stage_c(0, 0)
        stage_a(3, 1)
        stage_b(2, 0)
        stage_c(1, 1)

        def two_steps(i, _):
            n = 2 + 2 * i
            stage_a(n + 2, 0)
            stage_b(n + 1, 1)
            stage_c(n, 0)
            stage_a(n + 3, 1)
            stage_b(n + 2, 0)
            stage_c(n + 1, 1)
            return 0

        lax.fori_loop(0, qi - 1, two_steps, 0)
        last = ratio + n_full - 1
        stage_b(last, 1)
        stage_c(last - 1, 0)
        stage_c(last, 1)
        finish()


def _stick_breaking(p):
    bsz, _, t, _ = p.shape
    tq, tk = SB_Q_TILE, SB_K_TILE
    n_pair = GROUP_W // LANES
    idx = np.arange(tk)
    from_s = (idx[:, None] >= idx[None, :]).astype(np.float32)
    su = jnp.asarray(np.concatenate([from_s, from_s], axis=0), BF16)
    return pl.pallas_call(
        _sb_kernel,
        grid=(bsz, n_pair, t // tq),
        in_specs=[pl.BlockSpec((1, 1, tq, LANES), lambda bi, hi, qi: (bi, 4 * N_HEADS + hi, qi, 0)),
                  pl.BlockSpec((1, 1, t, LANES), lambda bi, hi, qi: (bi, 4 * N_HEADS + n_pair + hi, 0, 0)),
                  pl.BlockSpec((1, 1, t, LANES), lambda bi, hi, qi: (bi, 4 * N_HEADS + 2 * n_pair + hi, 0, 0)),
                  _const_spec((2 * tk, tk))],
        out_specs=pl.BlockSpec((1, tq, LANES), lambda bi, hi, qi: (bi, qi, hi)),
        out_shape=jax.ShapeDtypeStruct((bsz, t, GROUP_W), BF16),
        scratch_shapes=(2 * [pltpu.VMEM((2 * tq, tk), F32)] + 2 * [pltpu.VMEM((2 * tq, 2 * tk), BF16)]
                        + 2 * [pltpu.VMEM((2 * tq, tk), BF16)] + 4 * [pltpu.VMEM((2 * tq, LANES), F32)]),
        compiler_params=_cparams("arbitrary", "arbitrary", "arbitrary"),
        name="stick_breaking",
    )(p, p, p, su)


def _ffn_kernel(*refs, tm, final):
    if final:
        (x_ref, y1_ref, y2_ref, mod_ref, nw_ref, wo_ref, wu_ref, cw_ref, cb_ref, wd_ref, nf_ref,
         o_ref, ext_ref, carry_ref, act_ref, xs_ref) = refs
    else:
        (x_ref, y1_ref, y2_ref, mod_ref, nw_ref, wo_ref, wu_ref, cw_ref, cb_ref, wd_ref,
         o_ref, ext_ref, carry_ref, act_ref, xs_ref) = refs
    fc = FF_CHUNK
    n_chunks = D_FF // fc

    @pl.when(pl.program_id(1) == 0)
    def _():
        carry_ref[...] = jnp.zeros(carry_ref.shape, F32)

    depth = tm // SUBLANES
    halo = FFN_CONV - 1
    mod = mod_ref[0]
    mix = _dot(jnp.concatenate([y1_ref[0], y2_ref[0]], axis=1), wo_ref[...])
    n_lane_tiles = D_MODEL // LANES

    def restage(val, start_of, stride):
        for ct in range(n_lane_tiles):
            xs_ref[ct] = val[:, ct * LANES:(ct + 1) * LANES]
        return jnp.concatenate(
            [jnp.concatenate([xs_ref[ct, pl.ds(start_of(g), SUBLANES, stride=stride), :]
                              for ct in range(n_lane_tiles)], axis=1) for g in range(depth)], axis=0)

    x1 = restage(x_ref[0] + mod[2:3] * mix, lambda g: g, depth)
    hb = _modulated_norm_bf16([x1[r:r + NORM_STRIP] for r in range(0, tm, NORM_STRIP)],
                              nw_ref[...], mod[3:4], mod[4:5])
    acc = jnp.zeros((tm, D_MODEL), F32)
    sublane0 = lax.broadcasted_iota(jnp.int32, (SUBLANES, 2 * fc), 0) == 0

    def pair(ref, rows, j):
        return jnp.concatenate([ref[rows, j * fc:(j + 1) * fc], ref[rows, D_FF + j * fc:D_FF + (j + 1) * fc]],
                               axis=1)

    def up_project(j):
        ext = ext_ref.at[j % 2]
        u = _dot(hb, pair(wu_ref, slice(None), j))
        ext[halo * SUBLANES:halo * SUBLANES + tm, :] = u
        tail = u[tm - halo * SUBLANES:tm]
        prev = carry_ref[j]
        for r in range(halo):
            grp = slice(r * SUBLANES, (r + 1) * SUBLANES)
            ext[grp, :] = jnp.where(sublane0, pltpu.roll(prev[grp], 1, 0), pltpu.roll(tail[grp], 1, 0))
        carry_ref[j] = tail

    def down_project(acc, ks):
        return acc + _dot(act_ref[:, ks], wd_ref[ks, :])

    up_project(0)
    pending = None
    for j in range(n_chunks):
        if j + 1 < n_chunks:
            up_project(j + 1)
        if pending is not None:
            acc = down_project(acc, pending)
            pending = None
        ext = ext_ref.at[j % 2]
        u = pair(cb_ref, slice(None), j)
        for i in range(FFN_CONV):
            u = u + pair(cw_ref, slice(i, i + 1), j) * ext[i * SUBLANES:i * SUBLANES + tm, :]
        act_ref[:, j * fc:(j + 1) * fc] = (u[:, :fc] * _silu(u[:, fc:])).astype(BF16)
        if (j + 1) % FF_DOWN_GROUP == 0 or j + 1 == n_chunks:
            pending = slice((j // FF_DOWN_GROUP) * FF_DOWN_GROUP * fc, (j + 1) * fc)
    x2 = x1 + mod[5:6] * down_project(acc, pending)
    if final:
        x2 = _rms_lanes(x2) * nf_ref[...]
    o_ref[0] = restage(x2, lambda g: ((g * SUBLANES) % depth) * SUBLANES + (g * SUBLANES) // depth, SUBLANES)


def _ffn(x, y1, y2, mod_l, nw, w_out, w_up, conv_w, conv_b, w_down, norm_final=None):
    bsz, t, d = x.shape
    tm = min(TOK_TILE, t)
    final = norm_final is not None
    in_specs = [pl.BlockSpec((1, tm, d), lambda bi, ti: (bi, ti, 0)),
                pl.BlockSpec((1, tm, GROUP_W), lambda bi, ti: (bi, ti, 0)),
                pl.BlockSpec((1, tm, GROUP_W), lambda bi, ti: (bi, ti, 0)),
                pl.BlockSpec((1, 6, d), lambda bi, ti: (bi, 0, 0)),
                _const_spec((1, d)),
                _const_spec(w_out.shape), _const_spec(w_up.shape),
                _const_spec(conv_w.shape), _const_spec(conv_b.shape),
                _const_spec(w_down.shape)]
    args = [x, y1, y2, mod_l, nw, w_out, w_up, conv_w, conv_b, w_down]
    if final:
        in_specs.append(_const_spec((1, d)))
        args.append(norm_final)
    return pl.pallas_call(
        functools.partial(_ffn_kernel, tm=tm, final=final),
        grid=(bsz, t // tm),
        in_specs=in_specs,
        out_specs=pl.BlockSpec((1, tm, d), lambda bi, ti: (bi, ti, 0)),
        out_shape=jax.ShapeDtypeStruct((bsz, t, d), F32),
        scratch_shapes=[pltpu.VMEM((2, tm + (FFN_CONV - 1) * SUBLANES, 2 * FF_CHUNK), F32),
                        pltpu.VMEM((D_FF // FF_CHUNK, (FFN_CONV - 1) * SUBLANES, 2 * FF_CHUNK), F32),
                        pltpu.VMEM((tm, D_FF), BF16),
                        pltpu.VMEM((D_MODEL // LANES, tm, LANES), F32)],
        compiler_params=_cparams("arbitrary", "arbitrary"),
        name="outproj_ffn_final" if final else "outproj_ffn",
    )(*args)


def _head_rows(vec):
    return vec.reshape(-1, 1, LANES)


def kernel(x, c, w_ada, b_ada, norm_mix, norm_ffn, norm_final, w_in_even, b_in_even, conv_qk_w,
           conv_qk_b, mlstm_f_bias, mlstm_norm, hgrn_lb_logits, hgrn_norm, w_in_odd, b_in_odd,
           ret_norm, w_out, ffn_up, ffn_conv_w, ffn_conv_b, ffn_down):
    bsz, t, d = x.shape
    assert d == D_MODEL and t % SB_Q_TILE == 0 and t % REC_CHUNK == 0
    mod = _ada(c, w_ada, b_ada).reshape(DEPTH, bsz, 6, d)
    idx = np.arange(REC_CHUNK)
    tri = jnp.asarray((idx[None, :] <= idx[:, None]).astype(np.float32), BF16)
    gate0 = 4 * GROUP_W
    n_even = hgrn_lb_logits.shape[0]
    lb_rows = hgrn_lb_logits.astype(F32).reshape(n_even, 1, GROUP_W)

    for layer in range(DEPTH):
        mod_l = mod[layer]
        nw_mix = norm_mix[layer][None, :]
        if layer % 2 == 0:
            e = layer // 2
            w, b = w_in_even[e], b_in_even[e]
            w_main = jnp.concatenate([w[:, :gate0], w[:, gate0 + 2 * N_HEADS:]], axis=1).astype(BF16)
            b_main = jnp.concatenate([b[:gate0], b[gate0 + 2 * N_HEADS:]])[None, :]
            pad = LANES - 2 * N_HEADS
            gw = jnp.pad(w[:, gate0:gate0 + 2 * N_HEADS], ((0, 0), (0, pad))).astype(BF16)
            gb = jnp.pad(b[gate0:gate0 + 2 * N_HEADS], (0, pad))[None, :]
            p, gates = _inproj(x, mod_l, nw_mix, w_main, b_main, True, gw, gb,
                               conv_qk_w[e], conv_qk_b[e][None, :])
            fb_row = jnp.pad(mlstm_f_bias[e], (N_HEADS, pad))[None, :]
            y1 = _mlstm(p, gates, fb_row, _head_rows(mlstm_norm[e]), tri)
            y2 = _hgrn(p, lb_rows, _head_rows(hgrn_norm[e]), e)
        else:
            o = layer // 2
            (p,) = _inproj(x, mod_l, nw_mix, w_in_odd[o].astype(BF16), b_in_odd[o][None, :], False)
            y1 = _retention(p, _head_rows(ret_norm[o]))
            y2 = _stick_breaking(p)
        x = _ffn(x, y1, y2, mod_l, norm_ffn[layer][None, :], w_out[layer].astype(BF16),
                 ffn_up[layer].astype(BF16), ffn_conv_w[layer], ffn_conv_b[layer][None, :],
                 ffn_down[layer].astype(BF16),
                 norm_final[None, :] if layer == DEPTH - 1 else None)
    return x
```
